```python
import jax, jax.numpy as jnp
from jax import lax
import numpy as np

D_MODEL = 1024
BATCH = 2
SEQ = 8192
DEPTH = 2
DEC_BATCH = 8
DEC_SEQ = 8192
PAST_LEN = 128

N_META = 16
GRID_W = 64
EPS = 1e-6
POOL_WINDOWS = (2, 4, 8, 16)
POOL_WIDTH = D_MODEL // 2
POOL_GROUP = POOL_WIDTH // len(POOL_WINDOWS)
DN_DK = 128
DN_DV = 128
DN_HEADS = (D_MODEL // 2) // DN_DV
DN_CONV = 7
CHUNK = 64
DN_QK = DN_HEADS * DN_DK
DN_V = DN_HEADS * DN_DV
EVEN_IN = POOL_WIDTH + 2 * DN_QK + 2 * DN_V + 4 * DN_HEADS
EVEN_OUT = POOL_WIDTH + DN_V
ATT_HD = 128
ATT_HEADS = D_MODEL // ATT_HD
ATT_KV_HEADS = ATT_HEADS // 4
ATT_GROUP = ATT_HEADS // ATT_KV_HEADS
ROPE_THETA = 10000.0
ROPE_PAIRS_AXIS = ATT_HD // 4
Q_BLOCK = 128
ODD_IN = (ATT_HEADS + 2 * ATT_KV_HEADS) * ATT_HD
D_FF = 4 * D_MODEL
N_EVEN = (DEPTH + 1) // 2
N_ODD = DEPTH // 2

kernel_name = 'hybrid_pool_deltanet_gqa_encoder'

F32 = jnp.float32


def rms_norm(x, g):
    xf = x.astype(F32)
    y = xf * lax.rsqrt(jnp.mean(xf * xf, axis=-1, keepdims=True) + EPS)
    return (y * g.astype(F32)).astype(x.dtype)


def l2_norm(x):
    xf = x.astype(F32)
    return xf * lax.rsqrt(jnp.sum(xf * xf, axis=-1, keepdims=True) + EPS)


def pool_mixer(u, pool_w, pool_scale):
    bsz, n, _ = u.shape
    uf = u.astype(F32)
    cs = jnp.concatenate([jnp.zeros((bsz, 1, POOL_WIDTH), F32), jnp.cumsum(uf, axis=1)], axis=1)
    t = jnp.arange(n)
    means = []
    for gi, w in enumerate(POOL_WINDOWS):
        lo = jnp.clip(t - w // 2, 0, n - 1)
        hi = jnp.clip(t + (w - 1 - w // 2), 0, n - 1)
        csg = cs[..., gi * POOL_GROUP:(gi + 1) * POOL_GROUP]
        s = jnp.take(csg, hi + 1, axis=1) - jnp.take(csg, lo, axis=1)
        means.append(s / (hi - lo + 1).astype(F32)[None, :, None])
    d = (jnp.concatenate(means, axis=-1) - uf).astype(u.dtype)
    d = d.reshape(bsz, n, len(POOL_WINDOWS), POOL_GROUP)
    y = jnp.einsum('blgc,gcd->blgd', d, pool_w).reshape(bsz, n, POOL_WIDTH)
    return y * pool_scale


def gated_delta_chunked(q, k, v, beta, g):
    nb, t, h, dk = q.shape
    dv = v.shape[-1]
    nc = t // CHUNK

    def blk(a):
        a = a.astype(F32).reshape((nb, nc, CHUNK, h) + a.shape[3:])
        return jnp.moveaxis(a, 3, 1)

    q, k, v, beta, g = blk(q), blk(k), blk(v), blk(beta), blk(g)
    gc = jnp.cumsum(g, axis=-1)
    idx = jnp.arange(CHUNK)
    lower = idx[:, None] >= idx[None, :]
    strict = idx[:, None] > idx[None, :]
    decay = jnp.exp(jnp.where(lower, gc[..., :, None] - gc[..., None, :], -jnp.inf))
    kb = k * beta[..., None]
    lmat = jnp.where(strict, jnp.einsum('nhcid,nhcjd->nhcij', kb, k) * decay, 0.0)
    a_mat = lmat + jnp.eye(CHUNK, dtype=F32)
    u_c = lax.linalg.triangular_solve(a_mat, v * beta[..., None], left_side=True, lower=True, unit_diagonal=True)
    w_c = lax.linalg.triangular_solve(a_mat, kb * jnp.exp(gc)[..., None], left_side=True, lower=True, unit_diagonal=True)
    attn = jnp.einsum('nhcid,nhcjd->nhcij', q, k) * decay
    g_last = gc[..., -1]
    k_dec = k * jnp.exp(g_last[..., None] - gc)[..., None]
    q_dec = q * jnp.exp(gc)[..., None]
    xs = tuple(jnp.moveaxis(a, 2, 0) for a in (q_dec, k_dec, u_c, w_c, attn, g_last))

    def step(s, inp):
        qd, kd, uc, wc, at, gl = inp
        v_new = uc - jnp.einsum('nhcd,nhde->nhce', wc, s)
        o = jnp.einsum('nhcd,nhde->nhce', qd, s) + jnp.einsum('nhij,nhje->nhie', at, v_new)
        s = s * jnp.exp(gl)[..., None, None] + jnp.einsum('nhcd,nhce->nhde', kd, v_new)
        return s, o

    s0 = jnp.zeros((nb, h, dk, dv), F32)
    _, o = lax.scan(step, s0, xs)
    return jnp.transpose(o, (1, 0, 3, 2, 4)).reshape(nb, t, h, dv)


def delta_mixer(u_qkv, z, b, a, conv_w, a_log, dt_bias, norm_g):
    bsz, n, c = u_qkv.shape
    qkv = lax.conv_general_dilated(u_qkv, conv_w[:, None, :], (1,), [(DN_CONV // 2, DN_CONV // 2)],
                                   dimension_numbers=('NWC', 'WIO', 'NWC'), feature_group_count=c)
    qkv = jax.nn.silu(qkv)
    q = l2_norm(qkv[..., :DN_QK].reshape(bsz, n, DN_HEADS, DN_DK)) * (DN_DK ** -0.5)
    k = l2_norm(qkv[..., DN_QK:2 * DN_QK].reshape(bsz, n, DN_HEADS, DN_DK))
    v = qkv[..., 2 * DN_QK:].astype(F32).reshape(bsz, n, DN_HEADS, DN_DV)
    beta = jax.nn.sigmoid(b.astype(F32)).reshape(bsz, n, 2, DN_HEADS)
    g = -jnp.exp(a_log.astype(F32)) * jax.nn.softplus(a.astype(F32).reshape(bsz, n, 2, DN_HEADS) + dt_bias.astype(F32))
    pad = (-n) % CHUNK

    def pad_t(x, front):
        widths = [(0, 0), (pad, 0) if front else (0, pad)] + [(0, 0)] * (x.ndim - 2)
        return jnp.pad(x, widths)

    def both(xf, xb):
        return jnp.concatenate([pad_t(xf, True), pad_t(jnp.flip(xb, axis=1), False)], axis=0)

    o = gated_delta_chunked(both(q, q), both(k, k), both(v, v),
                            both(beta[:, :, 0], beta[:, :, 1]), both(g[:, :, 0], g[:, :, 1]))
    o = o[:bsz, pad:] + jnp.flip(o[bsz:, :n], axis=1)
    o = rms_norm(o, norm_g) * jax.nn.silu(z.astype(F32).reshape(bsz, n, DN_HEADS, DN_DV))
    return o.reshape(bsz, n, DN_V).astype(u_qkv.dtype)


def rope_tables(n_tokens):
    rows = n_tokens // GRID_W
    row = jnp.repeat(jnp.arange(rows), GRID_W).astype(F32)
    col = jnp.tile(jnp.arange(GRID_W), rows).astype(F32)
    freqs = ROPE_THETA ** (-(jnp.arange(ROPE_PAIRS_AXIS, dtype=F32) / ROPE_PAIRS_AXIS))
    ang = jnp.concatenate([row[:, None] * freqs, col[:, None] * freqs], axis=-1)
    ang = jnp.concatenate([jnp.zeros((N_META, 2 * ROPE_PAIRS_AXIS), F32), ang], axis=0)
    return jnp.cos(ang), jnp.sin(ang)


def apply_rope(x, cos, sin):
    half = x.shape[-1] // 2
    shp = (x.shape[1],) + (1,) * (x.ndim - 3) + (half,)
    c, s = cos.reshape(shp), sin.reshape(shp)
    xf = x.astype(F32)
    x1, x2 = xf[..., :half], xf[..., half:]
    return jnp.concatenate([x1 * c - x2 * s, x2 * c + x1 * s], axis=-1).astype(x.dtype)


def attention_mixer(u, q_norm, k_norm, cos, sin):
    bsz, n, _ = u.shape
    hq, hk = ATT_HEADS * ATT_HD, ATT_KV_HEADS * ATT_HD
    q = u[..., :hq].reshape(bsz, n, ATT_KV_HEADS, ATT_GROUP, ATT_HD)
    k = u[..., hq:hq + hk].reshape(bsz, n, ATT_KV_HEADS, ATT_HD)
    v = u[..., hq + hk:].reshape(bsz, n, ATT_KV_HEADS, ATT_HD)
    q = apply_rope(rms_norm(q, q_norm), cos, sin)
    k = apply_rope(rms_norm(k, k_norm), cos, sin)
    scale = ATT_HD ** -0.5

    def attend(qb):
        s = jnp.einsum('bqhgd,bkhd->bhgqk', qb, k).astype(F32) * scale
        p = jax.nn.softmax(s, axis=-1)
        return jnp.einsum('bhgqk,bkhd->bqhgd', p.astype(v.dtype), v)

    out_meta = attend(q[:, :N_META])
    n_real = n - N_META
    qr = jnp.moveaxis(q[:, N_META:].reshape(bsz, n_real // Q_BLOCK, Q_BLOCK, ATT_KV_HEADS, ATT_GROUP, ATT_HD), 1, 0)
    out_real = jnp.moveaxis(lax.map(attend, qr), 0, 1).reshape(bsz, n_real, ATT_KV_HEADS, ATT_GROUP, ATT_HD)
    return jnp.concatenate([out_meta, out_real], axis=1).reshape(bsz, n, hq)


def trunk(x, p):
    bsz, s, _ = x.shape
    h = jnp.concatenate([jnp.broadcast_to(p['meta_tokens'][None].astype(x.dtype), (bsz, N_META, D_MODEL)), x], axis=1)
    cos, sin = rope_tables(s)
    o1 = POOL_WIDTH
    o2 = o1 + 2 * DN_QK + DN_V
    o3 = o2 + DN_V
    o4 = o3 + 2 * DN_HEADS
    for i in range(DEPTH):
        xn = rms_norm(h, p['mix_norm'][i])
        if i % 2 == 0:
            j = i // 2
            u = xn @ p['w_in_even'][j]
            y_pool = pool_mixer(u[..., :o1], p['pool_w'][j], p['pool_scale'][j])
            y_dn = delta_mixer(u[..., o1:o2], u[..., o2:o3], u[..., o3:o4], u[..., o4:],
                               p['conv_qkv'][j], p['a_log'][j], p['dt_bias'][j], p['delta_norm'][j])
            mix = jnp.concatenate([y_pool.astype(h.dtype), y_dn.astype(h.dtype)], axis=-1) @ p['w_out_even'][j]
        else:
            j = i // 2
            u = xn @ p['w_in_odd'][j]
            mix = attention_mixer(u, p['q_norm'][j], p['k_norm'][j], cos, sin) @ p['w_out_odd'][j]
        h = h + mix.astype(h.dtype)
        hn = rms_norm(h, p['mlp_norm'][i])
        h = h + (jnp.square(jax.nn.relu(hn @ p['w_mlp_in'][i])) @ p['w_mlp_out'][i]).astype(h.dtype)
    return h[:, N_META:]


def setup_inputs(seed: int = 0) -> dict:
    key = jax.random.key(seed)
    ks = jax.random.split(key, 20)
    nrm = jax.random.normal
    dt = jnp.exp(jax.random.uniform(ks[10], (N_EVEN, 2, DN_HEADS), F32, np.log(1e-3), np.log(1e-1)))
    return {
        'x_prompt': nrm(ks[0], (BATCH, SEQ, D_MODEL), F32),
        'x_sample': nrm(ks[1], (DEC_BATCH, DEC_SEQ, D_MODEL), F32),
        'meta_tokens': nrm(ks[2], (N_META, D_MODEL), F32),
        'mix_norm': 1.0 + 0.02 * nrm(ks[3], (DEPTH, D_MODEL), F32),
        'mlp_norm': 1.0 + 0.02 * nrm(ks[4], (DEPTH, D_MODEL), F32),
        'w_in_even': nrm(ks[5], (N_EVEN, D_MODEL, EVEN_IN), F32) * D_MODEL ** -0.5,
        'pool_w': nrm(ks[6], (N_EVEN, len(POOL_WINDOWS), POOL_GROUP, POOL_GROUP), F32) * POOL_GROUP ** -0.5,
        'pool_scale': 1.0 + 0.1 * nrm(ks[7], (N_EVEN, POOL_WIDTH), F32),
        'conv_qkv': nrm(ks[8], (N_EVEN, DN_CONV, 2 * DN_QK + DN_V), F32) * DN_CONV ** -0.5,
        'a_log': jnp.log(jax.random.uniform(ks[9], (N_EVEN, 2, DN_HEADS), F32, 1.0, 16.0)),
        'dt_bias': dt + jnp.log(-jnp.expm1(-dt)),
        'delta_norm': 1.0 + 0.02 * nrm(ks[11], (N_EVEN, DN_DV), F32),
        'w_out_even': nrm(ks[12], (N_EVEN, EVEN_OUT, D_MODEL), F32) * EVEN_OUT ** -0.5,
        'w_in_odd': nrm(ks[13], (N_ODD, D_MODEL, ODD_IN), F32) * D_MODEL ** -0.5,
        'q_norm': 1.0 + 0.02 * nrm(ks[14], (N_ODD, ATT_HD), F32),
        'k_norm': 1.0 + 0.02 * nrm(ks[15], (N_ODD, ATT_HD), F32),
        'w_out_odd': nrm(ks[16], (N_ODD, ATT_HEADS * ATT_HD, D_MODEL), F32) * (ATT_HEADS * ATT_HD) ** -0.5,
        'w_mlp_in': nrm(ks[17], (DEPTH, D_MODEL, D_FF), F32) * D_MODEL ** -0.5,
        'w_mlp_out': nrm(ks[18], (DEPTH, D_FF, D_MODEL), F32) * D_FF ** -0.5,
    }


def reference(x_prompt, x_sample, meta_tokens, mix_norm, mlp_norm, w_in_even, pool_w, pool_scale, conv_qkv,
              a_log, dt_bias, delta_norm, w_out_even, w_in_odd, q_norm, k_norm, w_out_odd, w_mlp_in, w_mlp_out):
    params = {
        'meta_tokens': meta_tokens, 'mix_norm': mix_norm, 'mlp_norm': mlp_norm,
        'w_in_even': w_in_even, 'pool_w': pool_w, 'pool_scale': pool_scale, 'conv_qkv': conv_qkv,
        'a_log': a_log, 'dt_bias': dt_bias, 'delta_norm': delta_norm, 'w_out_even': w_out_even,
        'w_in_odd': w_in_odd, 'q_norm': q_norm, 'k_norm': k_norm, 'w_out_odd': w_out_odd,
        'w_mlp_in': w_mlp_in, 'w_mlp_out': w_mlp_out,
    }
    y_prompt = trunk(x_prompt, params)
    y_sample = trunk(x_sample, params)
    return (y_prompt, y_sample)
```

```python
import functools

import jax
import jax.numpy as jnp
import numpy as np
from jax import lax
from jax.experimental import pallas as pl
from jax.experimental.pallas import tpu as pltpu

F32 = jnp.float32
BF16 = jnp.bfloat16

D_MODEL = 1024
N_META = 16
LANE = 128
HALO = 16
FRONT = LANE - N_META
GRID_W = 64
EPS = 1e-6
POOL_WINDOWS = (2, 4, 8, 16)
POOL_WIDTH = 512
DN_HEADS = 4
DN_CONV = 7
CHUNK = 64
DN_QK = 512
DN_V = 512
ATT_HD = 128
ATT_HEADS = 8
ATT_KV_HEADS = 2
ATT_GROUP = 4
ROPE_THETA = 10000.0
D_FF = 4096
FF_CHUNK = 1024
VMEM_LIMIT = 56 * 1024 * 1024
NEG_BIG = -1e30


def _token_tile(lp):
    for t in (640, 512, 384, 256, 128):
        if lp % t == 0:
            return t
    raise ValueError(lp)


def _params(sem):
    return pltpu.CompilerParams(dimension_semantics=sem, vmem_limit_bytes=VMEM_LIMIT)


def _const_spec(shape):
    nd = len(shape)
    return pl.BlockSpec(shape, lambda *_: (0,) * nd, pipeline_mode=pl.Buffered(1))


def _dot(a, b):
    return jnp.dot(a, b, preferred_element_type=F32)


def _dot_nt(a, b):
    return lax.dot_general(a, b, (((1,), (1,)), ((), ())), preferred_element_type=F32)


def _dot_tn(a, b):
    return lax.dot_general(a, b, (((0,), (0,)), ((), ())), preferred_element_type=F32)


def _split2(x):
    hi = x.astype(BF16)
    lo = (x - hi.astype(F32)).astype(BF16)
    return hi, lo


def _split3(x):
    hi = x.astype(BF16)
    r = x - hi.astype(F32)
    mid = r.astype(BF16)
    lo = (r - mid.astype(F32)).astype(BF16)
    return hi, mid, lo


def _lane_sum(x, ones_bf):
    hi, lo = _split2(x)
    return _dot(hi, ones_bf) + _dot(lo, ones_bf)


def _sigmoid(x):
    return 1.0 / (1.0 + jnp.exp(-x))


def _rms_rows(x, g):
    ms = jnp.mean(x * x, axis=-1, keepdims=True)
    return x * lax.rsqrt(ms + EPS) * g


def _in_even_kernel(h_ref, g_ref, wp_ref, wq_ref, wz_ref, wb_ref, up_ref, uq_ref, z_ref, ba_ref):
    xn = _rms_rows(h_ref[...], g_ref[...]).astype(BF16)
    up_ref[...] = _dot(xn, wp_ref[...]).astype(BF16)
    uq_ref[...] = _dot(xn, wq_ref[...]).astype(BF16)
    z_ref[...] = _dot(xn, wz_ref[...]).astype(BF16)
    ba_ref[...] = _dot(xn, wb_ref[...])


def _in_even(h2, g, wp, wq, wz, wb, tt):
    r = h2.shape[0]
    row = lambda n: pl.BlockSpec((tt, n), lambda i: (i, 0))
    return pl.pallas_call(
        _in_even_kernel,
        grid=(r // tt,),
        in_specs=[row(D_MODEL), _const_spec((1, D_MODEL)), _const_spec(wp.shape), _const_spec(wq.shape),
                  _const_spec(wz.shape), _const_spec(wb.shape)],
        out_specs=[row(POOL_WIDTH), row(2 * DN_QK + DN_V), row(DN_V), row(LANE)],
        out_shape=[jax.ShapeDtypeStruct((r, POOL_WIDTH), BF16), jax.ShapeDtypeStruct((r, 2 * DN_QK + DN_V), BF16),
                   jax.ShapeDtypeStruct((r, DN_V), BF16), jax.ShapeDtypeStruct((r, LANE), F32)],
        compiler_params=_params(("parallel",)),
        name="in_proj_even",
    )(h2, g, wp, wq, wz, wb)


def _prep_kernel(up_ref, upl_ref, upr_ref, uq_ref, uql_ref, uqr_ref, ba_ref, band_ref, tri_ref, pw_ref, ps_ref,
                 cw_ref, gp_ref, yp_ref, q_ref, k_ref, v_ref, gt_ref, ext_ref, *, ts, lp):
    i = pl.program_id(1)
    row0 = i * ts
    rows_ext = lax.broadcasted_iota(jnp.int32, (ts + 2 * HALO, LANE), 0) + (row0 - HALO)
    valid_ext = (rows_ext >= FRONT) & (rows_ext < lp)
    valid_main = valid_ext[HALO:HALO + ts]
    ones_bf = jnp.ones((LANE, LANE), BF16)

    validb = jnp.where(valid_ext, 1.0, 0.0).astype(BF16)
    for gi in range(len(POOL_WINDOWS)):
        sl = slice(gi * LANE, (gi + 1) * LANE)
        eg = jnp.concatenate([upl_ref[0, :, sl], up_ref[0, :, sl], upr_ref[0, :, sl]], axis=0)
        eg = jnp.where(valid_ext, eg, jnp.zeros_like(eg))
        sc = _dot(band_ref[gi], jnp.concatenate([eg, validb], axis=1))
        mean = sc[:, :LANE] / jnp.maximum(sc[:, LANE:], 1.0)
        d = (mean - up_ref[0, :, sl].astype(F32)).astype(BF16)
        y = _dot(d, pw_ref[gi]) * ps_ref[:, sl]
        yp_ref[0, :, sl] = jnp.where(valid_main, y, 0.0).astype(BF16)

    n_blk = (2 * DN_QK + DN_V) // LANE
    for cb in range(n_blk):
        sl = slice(cb * LANE, (cb + 1) * LANE)
        e = jnp.concatenate([uql_ref[0, :, sl], uq_ref[0, :, sl], uqr_ref[0, :, sl]], axis=0).astype(F32)
        ext_ref[...] = jnp.where(valid_ext, e, 0.0)
        acc = jnp.zeros((ts, LANE), F32)
        for j in range(DN_CONV):
            acc = acc + cw_ref[j:j + 1, sl] * ext_ref[pl.ds(HALO - DN_CONV // 2 + j, ts), :]
        x = acc * _sigmoid(acc)
        x = jnp.where(valid_main, x, 0.0)
        if cb < 2 * DN_HEADS:
            ss = _lane_sum(x * x, ones_bf)
            x = x * lax.rsqrt(ss + EPS)
        if cb < DN_HEADS:
            q_ref[0, :, sl] = (x * (LANE ** -0.5)).astype(BF16)
        elif cb < 2 * DN_HEADS:
            k_ref[0, :, slice((cb - DN_HEADS) * LANE, (cb - DN_HEADS + 1) * LANE)] = x.astype(BF16)
        else:
            v_ref[0, :, slice((cb - 2 * DN_HEADS) * LANE, (cb - 2 * DN_HEADS + 1) * LANE)] = x.astype(BF16)

    ba = ba_ref[0]
    lane = lax.broadcasted_iota(jnp.int32, (ts, LANE), 1)
    beta = jnp.where(valid_main, _sigmoid(ba), 0.0)
    xs = ba + gp_ref[1:2, :]
    softplus = jnp.maximum(xs, 0.0) + jnp.log1p(jnp.exp(-jnp.abs(xs)))
    g = jnp.where(valid_main, gp_ref[0:1, :] * softplus, 0.0)
    g3 = _split3(g)
    gcf = _dot(tri_ref[0], g3[0]) + _dot(tri_ref[0], g3[1]) + _dot(tri_ref[0], g3[2])
    gcb = _dot(tri_ref[1], g3[0]) + _dot(tri_ref[1], g3[1]) + _dot(tri_ref[1], g3[2])
    gt_ref[0] = jnp.where(lane < 2 * DN_HEADS, beta,
                          jnp.where(lane < 3 * DN_HEADS, gcf, jnp.where(lane < 4 * DN_HEADS, gcb, 0.0)))


def _prep_constants(ts):
    t = np.arange(ts)[:, None]
    j = np.arange(ts + 2 * HALO)[None, :] - HALO
    bands = []
    for w in POOL_WINDOWS:
        bands.append(((j >= t - w // 2) & (j <= t + (w - 1 - w // 2))).astype(np.float32))
    band = jnp.asarray(np.stack(bands), BF16)
    a = np.arange(ts)
    same = (a[:, None] // CHUNK) == (a[None, :] // CHUNK)
    tri = np.stack([same & (a[None, :] <= a[:, None]), same & (a[None, :] >= a[:, None])]).astype(np.float32)
    return band, jnp.asarray(tri, BF16)


def _prep(up, uq, ba, pool_w, pool_scale, conv_w, gate_p, ts):
    b, lp, _ = up.shape
    nt = lp // ts
    hb = ts // HALO
    band, tri = _prep_constants(ts)
    cqkv = 2 * DN_QK + DN_V
    main = lambda n: pl.BlockSpec((1, ts, n), lambda bi, i: (bi, i, 0))
    left = lambda n: pl.BlockSpec((1, HALO, n), lambda bi, i: (bi, jnp.maximum(i * hb - 1, 0), 0))
    right = lambda n: pl.BlockSpec((1, HALO, n), lambda bi, i: (bi, jnp.minimum((i + 1) * hb, lp // HALO - 1), 0))
    out = lambda n, dt: jax.ShapeDtypeStruct((b, lp, n), dt)
    return pl.pallas_call(
        functools.partial(_prep_kernel, ts=ts, lp=lp),
        grid=(b, nt),
        in_specs=[main(POOL_WIDTH), left(POOL_WIDTH), right(POOL_WIDTH), main(cqkv), left(cqkv), right(cqkv),
                  main(LANE), _const_spec(band.shape), _const_spec(tri.shape), _const_spec(pool_w.shape),
                  _const_spec(pool_scale.shape), _const_spec(conv_w.shape), _const_spec(gate_p.shape)],
        out_specs=[main(POOL_WIDTH), main(DN_QK), main(DN_QK), main(DN_V), main(LANE)],
        out_shape=[out(POOL_WIDTH, BF16), out(DN_QK, BF16), out(DN_QK, BF16), out(DN_V, BF16), out(LANE, F32)],
        scratch_shapes=[pltpu.VMEM((ts + 2 * HALO, LANE), F32)],
        compiler_params=_params(("parallel", "parallel")),
        name="mixer_prep",
    )(up, up, up, uq, uq, uq, ba, band, tri, pool_w, pool_scale, conv_w, gate_p)


def _unit_tri_inverse(lmat, m16, off_masks):
    eye = (lax.broadcasted_iota(jnp.int32, (CHUNK, CHUNK), 0)
           == lax.broadcasted_iota(jnp.int32, (CHUNK, CHUNK), 1)).astype(F32)
    n = -jnp.where(m16, lmat, 0.0)
    t = eye + n
    p = n
    for _ in range(3):
        pb = p.astype(BF16)
        p = _dot(pb, pb)
        t = t + _dot(t.astype(BF16), p.astype(BF16))
    for lo_mask in off_masks:
        tb = t.astype(BF16)
        off = jnp.where(lo_mask, lmat, 0.0).astype(BF16)
        t = t - _dot(_dot(tb, off).astype(BF16), tb)
    return t


def _delta_chunk(s, qc, kc, vc, gt, d, masks):
    incl, strict, m16, off_masks = masks[d]
    beta = gt[:, 4 * d:4 * d + 1]
    gc = gt[:, 8 + 4 * d:9 + 4 * d]
    gtt = gt.T
    gcr = gtt[8 + 4 * d:9 + 4 * d, :]
    decay = jnp.exp(jnp.where(incl, gc - gcr, -jnp.inf))
    kf = kc.astype(F32)
    kbf = kf * beta
    raw = _dot_nt(jnp.concatenate([kbf.astype(BF16), qc], axis=0), kc)
    lmat = jnp.where(strict, raw[:CHUNK] * decay, 0.0)
    attn = raw[CHUNK:] * decay
    t = _unit_tri_inverse(lmat, m16, off_masks)
    egc = jnp.exp(gc)
    rhs = jnp.concatenate([vc.astype(F32) * beta, kbf * egc], axis=1).astype(BF16)
    uw = _dot(t.astype(BF16), rhs)
    u, w = uw[:, :LANE], uw[:, LANE:]
    qd = (qc.astype(F32) * egc).astype(BF16)
    ws_qs = _dot(jnp.concatenate([w.astype(BF16), qd], axis=0), s.astype(BF16))
    v_new = u - ws_qs[:CHUNK]
    v_new_bf = v_new.astype(BF16)
    o = ws_qs[CHUNK:] + _dot(attn.astype(BF16), v_new_bf)
    g_last = gc[CHUNK - 1:CHUNK, :] if d == 0 else gc[0:1, :]
    kd = (kf * jnp.exp(g_last - gc)).astype(BF16)
    s_new = s * jnp.exp(g_last) + _dot_tn(kd, v_new_bf)
    return s_new, o


def _delta_kernel(q_ref, k_ref, v_ref, gt_ref, z_ref, ng_ref, y_ref, of_ref, ob_ref, *, lp, ts):
    hd = pl.program_id(1)
    nc = lp // CHUNK
    r_i = lax.broadcasted_iota(jnp.int32, (CHUNK, CHUNK), 0)
    c_i = lax.broadcasted_iota(jnp.int32, (CHUNK, CHUNK), 1)
    m16 = (r_i // 16) == (c_i // 16)
    m32 = (r_i // 32) == (c_i // 32)
    off_masks = (m32 & ~m16, ~m32)
    masks = ((r_i >= c_i, r_i > c_i, m16, off_masks), (r_i <= c_i, r_i < c_i, m16, off_masks))

    def load(c):
        r = pl.multiple_of(c * CHUNK, CHUNK)
        rows = pl.ds(r, CHUNK)
        gt = pltpu.roll(gt_ref[0, rows, :], LANE - hd, axis=1)
        return rows, q_ref[0, rows, :], k_ref[0, rows, :], v_ref[0, rows, :], gt

    def body(t, carry):
        sf, sb = carry
        rows, qc, kc, vc, gt = load(1 + t)
        sf, o = _delta_chunk(sf, qc, kc, vc, gt, 0, masks)
        of_ref[rows, :] = o
        rows, qc, kc, vc, gt = load(nc - 1 - t)
        sb, o = _delta_chunk(sb, qc, kc, vc, gt, 1, masks)
        ob_ref[rows, :] = o
        return sf, sb

    s0 = jnp.zeros((LANE, LANE), F32)
    lax.fori_loop(0, nc - 1, body, (s0, s0))

    of_ref[0:CHUNK, :] = jnp.zeros((CHUNK, LANE), F32)
    ob_ref[0:CHUNK, :] = jnp.zeros((CHUNK, LANE), F32)

    def epilogue(i, _):
        rows = pl.ds(pl.multiple_of(i * ts, ts), ts)
        o = of_ref[rows, :] + ob_ref[rows, :]
        zf = z_ref[0, rows, :].astype(F32)
        y_ref[0, rows, :] = (_rms_rows(o, ng_ref[...]) * (zf * _sigmoid(zf))).astype(BF16)
        return 0

    lax.fori_loop(0, lp // ts, epilogue, 0)


def _delta(q, k, v, gt, z, norm_g, ts):
    b, lp, _ = q.shape
    head = pl.BlockSpec((1, lp, LANE), lambda bi, hi: (bi, 0, hi))
    return pl.pallas_call(
        functools.partial(_delta_kernel, lp=lp, ts=ts),
        grid=(b, DN_HEADS),
        in_specs=[head, head, head, pl.BlockSpec((1, lp, LANE), lambda bi, hi: (bi, 0, 0)), head,
                  _const_spec((1, LANE))],
        out_specs=head,
        out_shape=jax.ShapeDtypeStruct((b, lp, DN_V), BF16),
        scratch_shapes=[pltpu.VMEM((lp, LANE), F32), pltpu.VMEM((lp, LANE), F32)],
        compiler_params=_params(("parallel", "parallel")),
        name="delta_rule",
    )(q, k, v, gt, z, norm_g)


def _out_mlp_kernel(*refs, n_in):
    a_refs = refs[:n_in]
    wo_refs = refs[n_in:2 * n_in]
    h_ref, g_ref, w1_ref, w2_ref, o_ref = refs[2 * n_in:]
    h1 = h_ref[...]
    for a_ref, wo_ref in zip(a_refs, wo_refs):
        h1 = h1 + _dot(a_ref[...], wo_ref[...])
    hn = _rms_rows(h1, g_ref[...]).astype(BF16)
    acc = jnp.zeros_like(h1)
    for c in range(D_FF // FF_CHUNK):
        sl = slice(c * FF_CHUNK, (c + 1) * FF_CHUNK)
        hid = jnp.maximum(_dot(hn, w1_ref[:, sl]), 0.0)
        acc = acc + _dot((hid * hid).astype(BF16), w2_ref[sl, :])
    o_ref[...] = h1 + acc


def _out_mlp(a_list, wo_list, h2, g, w1, w2, tt):
    r = h2.shape[0]
    row = lambda n: pl.BlockSpec((tt, n), lambda i: (i, 0))
    return pl.pallas_call(
        functools.partial(_out_mlp_kernel, n_in=len(a_list)),
        grid=(r // tt,),
        in_specs=[row(a.shape[1]) for a in a_list] + [_const_spec(w.shape) for w in wo_list]
        + [row(D_MODEL), _const_spec((1, D_MODEL)), _const_spec(w1.shape), _const_spec(w2.shape)],
        out_specs=row(D_MODEL),
        out_shape=jax.ShapeDtypeStruct((r, D_MODEL), F32),
        compiler_params=_params(("parallel",)),
        name="out_proj_mlp",
    )(*a_list, *wo_list, h2, g, w1, w2)


def _in_odd_kernel(h_ref, g_ref, wq_ref, wk_ref, wv_ref, qn_ref, kn_ref, cos_ref, sin_ref, q_ref, k_ref, v_ref):
    xn = _rms_rows(h_ref[...], g_ref[...]).astype(BF16)
    ones_bf = jnp.ones((LANE, LANE), BF16)
    cos = cos_ref[...]
    sin = sin_ref[...]

    def norm_rope(x, gain, scale):
        ms = _lane_sum(x * x, ones_bf) * (1.0 / ATT_HD)
        y = x * lax.rsqrt(ms + EPS) * gain
        return (y * cos + pltpu.roll(y, ATT_HD // 2, axis=1) * sin) * scale

    uq = _dot(xn, wq_ref[...])
    for hh in range(ATT_HEADS):
        sl = slice(hh * ATT_HD, (hh + 1) * ATT_HD)
        q_ref[:, sl] = norm_rope(uq[:, sl], qn_ref[...], ATT_HD ** -0.5).astype(BF16)
    uk = _dot(xn, wk_ref[...])
    for hh in range(ATT_KV_HEADS):
        sl = slice(hh * ATT_HD, (hh + 1) * ATT_HD)
        k_ref[:, sl] = norm_rope(uk[:, sl], kn_ref[...], 1.0).astype(BF16)
    v_ref[...] = _dot(xn, wv_ref[...]).astype(BF16)


def _in_odd(h2, g, wq, wk, wv, qn, kn, cos, sin, tt):
    r = h2.shape[0]
    ntb = cos.shape[0] // tt
    row = lambda n: pl.BlockSpec((tt, n), lambda i: (i, 0))
    tab = pl.BlockSpec((tt, ATT_HD), lambda i: (i % ntb, 0))
    nq, nk = ATT_HEADS * ATT_HD, ATT_KV_HEADS * ATT_HD
    return pl.pallas_call(
        _in_odd_kernel,
        grid=(r // tt,),
        in_specs=[row(D_MODEL), _const_spec((1, D_MODEL)), _const_spec(wq.shape), _const_spec(wk.shape),
                  _const_spec(wv.shape), _const_spec((1, ATT_HD)), _const_spec((1, ATT_HD)), tab, tab],
        out_specs=[row(nq), row(nk), row(nk)],
        out_shape=[jax.ShapeDtypeStruct((r, nq), BF16), jax.ShapeDtypeStruct((r, nk), BF16),
                   jax.ShapeDtypeStruct((r, nk), BF16)],
        compiler_params=_params(("parallel",)),
        name="in_proj_odd",
    )(h2, g, wq, wk, wv, qn, kn, cos, sin)


def _attn_kernel(q_ref, k_ref, v_ref, o_ref, *, tqb, tk, lp):
    sub = LANE
    n_kc = lp // tk
    col = lax.broadcasted_iota(jnp.int32, (ATT_GROUP * sub, tk), 1)

    def step(c, carry, qs, masked):
        m, l, acc = carry
        rows = pl.ds(c * tk, tk) if isinstance(c, int) else pl.ds(pl.multiple_of(c * tk, tk), tk)
        s = _dot_nt(qs, k_ref[0, rows, :])
        if masked:
            s = jnp.where(col >= FRONT, s, -jnp.inf)
        m_new = jnp.maximum(m, jnp.max(s, axis=1, keepdims=True))
        p = jnp.exp(s - m_new)
        alpha = jnp.exp(m - m_new)
        l = alpha * l + jnp.sum(p, axis=1, keepdims=True)
        acc = alpha * acc + _dot(p.astype(BF16), v_ref[0, rows, :])
        return m_new, l, acc

    for si in range(tqb // sub):
        q4 = q_ref[0, si * sub:(si + 1) * sub, :]
        qs = jnp.concatenate([q4[:, g * ATT_HD:(g + 1) * ATT_HD] for g in range(ATT_GROUP)], axis=0)
        init = (jnp.full((ATT_GROUP * sub, 1), -jnp.inf, F32), jnp.zeros((ATT_GROUP * sub, 1), F32),
                jnp.zeros((ATT_GROUP * sub, ATT_HD), F32))
        carry = step(0, init, qs, True)
        m, l, acc = lax.fori_loop(1, n_kc, lambda c, cr: step(c, cr, qs, False), carry)
        out = (acc / l).astype(BF16)
        for g in range(ATT_GROUP):
            o_ref[0, si * sub:(si + 1) * sub, g * ATT_HD:(g + 1) * ATT_HD] = out[g * sub:(g + 1) * sub]


def _attention(q, k, v, tqb):
    b, lp, _ = q.shape
    gw = ATT_GROUP * ATT_HD
    qspec = pl.BlockSpec((1, tqb, gw), lambda bi, hi, i: (bi, i, hi))
    kvspec = pl.BlockSpec((1, lp, ATT_HD), lambda bi, hi, i: (bi, 0, hi))
    return pl.pallas_call(
        functools.partial(_attn_kernel, tqb=tqb, tk=tqb, lp=lp),
        grid=(b, ATT_KV_HEADS, lp // tqb),
        in_specs=[qspec, kvspec, kvspec],
        out_specs=qspec,
        out_shape=jax.ShapeDtypeStruct((b, lp, ATT_HEADS * ATT_HD), BF16),
        compiler_params=_params(("parallel", "parallel", "arbitrary")),
        name="gqa_attention",
    )(q, k, v)


def _rope_tables(lp):
    s = lp - LANE
    pos = np.arange(s)
    freqs = jnp.asarray(ROPE_THETA, F32) ** (-(jnp.arange(ATT_HD // 4, dtype=F32) / (ATT_HD // 4)))
    row = jnp.asarray(pos // GRID_W, F32)
    colp = jnp.asarray(pos % GRID_W, F32)
    ang = jnp.concatenate([row[:, None] * freqs, colp[:, None] * freqs], axis=-1)
    ang = jnp.concatenate([jnp.zeros((LANE, ATT_HD // 2), F32), ang], axis=0)
    c, sn = jnp.cos(ang), jnp.sin(ang)
    return jnp.concatenate([c, c], axis=-1), jnp.concatenate([-sn, sn], axis=-1)


def _trunk(x, meta_tokens, mix_norm, mlp_norm, w_in_even, pool_w, pool_scale, conv_qkv, a_log, dt_bias,
           delta_norm, w_out_even, w_in_odd, q_norm, k_norm, w_out_odd, w_mlp_in, w_mlp_out):
    b, s, _ = x.shape
    lp = s + LANE
    tt = _token_tile(lp)
    r = b * lp
    h = jnp.concatenate([jnp.zeros((b, FRONT, D_MODEL), F32),
                         jnp.broadcast_to(meta_tokens[None], (b, N_META, D_MODEL)), x], axis=1)
    h2 = h.reshape(r, D_MODEL)

    o1 = POOL_WIDTH
    o2 = o1 + 2 * DN_QK + DN_V
    o3 = o2 + DN_V
    w_in = w_in_even[0]
    wb = jnp.zeros((D_MODEL, LANE), F32).at[:, :4 * DN_HEADS].set(w_in[:, o3:])
    up, uq, z, ba = _in_even(h2, mix_norm[0][None], w_in[:, :o1].astype(BF16), w_in[:, o1:o2].astype(BF16),
                             w_in[:, o2:o3].astype(BF16), wb.astype(BF16), tt)
    gate_p = jnp.zeros((8, LANE), F32)
    gate_p = gate_p.at[0, 2 * DN_HEADS:4 * DN_HEADS].set(-jnp.exp(a_log[0].reshape(-1)))
    gate_p = gate_p.at[1, 2 * DN_HEADS:4 * DN_HEADS].set(dt_bias[0].reshape(-1))
    conv_w = jnp.zeros((8, 2 * DN_QK + DN_V), F32).at[:DN_CONV].set(conv_qkv[0])
    seq = lambda a: a.reshape(b, lp, a.shape[-1])
    y_pool, q, k, v, gt = _prep(seq(up), seq(uq), seq(ba), pool_w[0].astype(BF16), pool_scale[0][None], conv_w,
                                gate_p, tt)
    y_dn = _delta(q, k, v, gt, seq(z), delta_norm[0][None], tt)
    wo = w_out_even[0].astype(BF16)
    h2 = _out_mlp([y_pool.reshape(r, -1), y_dn.reshape(r, -1)], [wo[:POOL_WIDTH], wo[POOL_WIDTH:]], h2,
                  mlp_norm[0][None], w_mlp_in[0].astype(BF16), w_mlp_out[0].astype(BF16), tt)

    nq, nk = ATT_HEADS * ATT_HD, ATT_KV_HEADS * ATT_HD
    w_in = w_in_odd[0].astype(BF16)
    cos, sin = _rope_tables(lp)
    qa, ka, va = _in_odd(h2, mix_norm[1][None], w_in[:, :nq], w_in[:, nq:nq + nk], w_in[:, nq + nk:],
                         q_norm[0][None], k_norm[0][None], cos, sin, tt)
    att = _attention(seq(qa), seq(ka), seq(va), tt)
    h2 = _out_mlp([att.reshape(r, -1)], [w_out_odd[0].astype(BF16)], h2, mlp_norm[1][None],
                  w_mlp_in[1].astype(BF16), w_mlp_out[1].astype(BF16), tt)
    return h2.reshape(b, lp, D_MODEL)[:, LANE:]


def kernel(x_prompt, x_sample, meta_tokens, mix_norm, mlp_norm, w_in_even, pool_w, pool_scale, conv_qkv, a_log,
           dt_bias, delta_norm, w_out_even, w_in_odd, q_norm, k_norm, w_out_odd, w_mlp_in, w_mlp_out):
    nb = x_prompt.shape[0]
    x = jnp.concatenate([x_prompt, x_sample], axis=0)
    y = _trunk(x, meta_tokens, mix_norm, mlp_norm, w_in_even, pool_w, pool_scale, conv_qkv, a_log, dt_bias,
               delta_norm, w_out_even, w_in_odd, q_norm, k_norm, w_out_odd, w_mlp_in, w_mlp_out)
    return y[:nb], y[nb:]
```

```python
import functools
import math

import jax
import jax.numpy as jnp
import numpy as np
from jax import lax
from jax.experimental import pallas as pl
from jax.experimental.pallas import tpu as pltpu

F32 = jnp.float32
BF16 = jnp.bfloat16

D_MODEL = 1024
N_META = 16
LANE = 128
HALO = 16
FRONT = LANE - N_META
GRID_W = 64
EPS = 1e-6
POOL_WINDOWS = (2, 4, 8, 16)
POOL_WIDTH = 512
DN_HEADS = 4
DN_CONV = 7
CHUNK = 64
PAIR = 2 * CHUNK
DN_QK = 512
DN_V = 512
ATT_HD = 128
ATT_HEADS = 8
ATT_KV_HEADS = 2
ATT_GROUP = 4
ROPE_THETA = 10000.0
D_FF = 4096
FF_CHUNK = 1024
VMEM_LIMIT = 56 * 1024 * 1024
LOG2E = math.log2(math.e)
SOFTMAX_MIN_DENOM = 1e-26


def _token_tile(lp):
    for t in (640, 512, 384, 256, 128):
        if lp % t == 0:
            return t
    raise ValueError(lp)


def _params(sem):
    return pltpu.CompilerParams(dimension_semantics=sem, vmem_limit_bytes=VMEM_LIMIT)


def _const_spec(shape):
    nd = len(shape)
    return pl.BlockSpec(shape, lambda *_: (0,) * nd, pipeline_mode=pl.Buffered(1))


def _dot(a, b):
    return jnp.dot(a, b, preferred_element_type=F32)


def _dot_nt(a, b):
    return lax.dot_general(a, b, (((1,), (1,)), ((), ())), preferred_element_type=F32)


def _dot_tn(a, b):
    return lax.dot_general(a, b, (((0,), (0,)), ((), ())), preferred_element_type=F32)


def _split2(x):
    hi = x.astype(BF16)
    lo = (x - hi.astype(F32)).astype(BF16)
    return hi, lo


def _split3(x):
    hi = x.astype(BF16)
    r = x - hi.astype(F32)
    mid = r.astype(BF16)
    lo = (r - mid.astype(F32)).astype(BF16)
    return hi, mid, lo


def _lane_sum(x, ones_bf):
    hi, lo = _split2(x)
    return _dot(hi, ones_bf) + _dot(lo, ones_bf)


def _sigmoid(x):
    return 1.0 / (1.0 + jnp.exp(-x))


def _rms_rows(x, g):
    ms = jnp.mean(x * x, axis=-1, keepdims=True)
    return x * lax.rsqrt(ms + EPS) * g


def _in_even_kernel(h_ref, g_ref, wp_ref, wq_ref, wz_ref, wb_ref, up_ref, uq_ref, z_ref, ba_ref):
    xn = _rms_rows(h_ref[...], g_ref[...]).astype(BF16)
    up_ref[...] = _dot(xn, wp_ref[...]).astype(BF16)
    uq_ref[...] = _dot(xn, wq_ref[...]).astype(BF16)
    z_ref[...] = _dot(xn, wz_ref[...]).astype(BF16)
    ba_ref[...] = _dot(xn, wb_ref[...])


def _in_even(h2, g, wp, wq, wz, wb, tt):
    r = h2.shape[0]
    row = lambda n: pl.BlockSpec((tt, n), lambda i: (i, 0))
    return pl.pallas_call(
        _in_even_kernel,
        grid=(r // tt,),
        in_specs=[row(D_MODEL), _const_spec((1, D_MODEL)), _const_spec(wp.shape), _const_spec(wq.shape),
                  _const_spec(wz.shape), _const_spec(wb.shape)],
        out_specs=[row(POOL_WIDTH), row(2 * DN_QK + DN_V), row(DN_V), row(LANE)],
        out_shape=[jax.ShapeDtypeStruct((r, POOL_WIDTH), BF16), jax.ShapeDtypeStruct((r, 2 * DN_QK + DN_V), BF16),
                   jax.ShapeDtypeStruct((r, DN_V), BF16), jax.ShapeDtypeStruct((r, LANE), F32)],
        compiler_params=_params(("parallel",)),
        name="in_proj_even",
    )(h2, g, wp, wq, wz, wb)


def _prep_kernel(up_ref, upl_ref, upr_ref, uq_ref, uql_ref, uqr_ref, ba_ref, band_ref, tri_ref, pw_ref, ps_ref,
                 cw_ref, gp_ref, yp_ref, q_ref, k_ref, v_ref, gt_ref, ext_ref, *, ts, lp):
    i = pl.program_id(1)
    row0 = i * ts
    rows_ext = lax.broadcasted_iota(jnp.int32, (ts + 2 * HALO, LANE), 0) + (row0 - HALO)
    valid_ext = (rows_ext >= FRONT) & (rows_ext < lp)
    valid_main = valid_ext[HALO:HALO + ts]
    ones_bf = jnp.ones((LANE, LANE), BF16)

    validb = jnp.where(valid_ext, 1.0, 0.0).astype(BF16)
    for gi in range(len(POOL_WINDOWS)):
        sl = slice(gi * LANE, (gi + 1) * LANE)
        eg = jnp.concatenate([upl_ref[0, :, sl], up_ref[0, :, sl], upr_ref[0, :, sl]], axis=0)
        eg = jnp.where(valid_ext, eg, jnp.zeros_like(eg))
        sc = _dot(band_ref[gi], jnp.concatenate([eg, validb], axis=1))
        mean = sc[:, :LANE] / jnp.maximum(sc[:, LANE:], 1.0)
        d = (mean - up_ref[0, :, sl].astype(F32)).astype(BF16)
        y = _dot(d, pw_ref[gi]) * ps_ref[:, sl]
        yp_ref[0, :, sl] = jnp.where(valid_main, y, 0.0).astype(BF16)

    n_blk = (2 * DN_QK + DN_V) // LANE
    for cb in range(n_blk):
        sl = slice(cb * LANE, (cb + 1) * LANE)
        e = jnp.concatenate([uql_ref[0, :, sl], uq_ref[0, :, sl], uqr_ref[0, :, sl]], axis=0).astype(F32)
        ext_ref[...] = jnp.where(valid_ext, e, 0.0)
        acc = jnp.zeros((ts, LANE), F32)
        for j in range(DN_CONV):
            acc = acc + cw_ref[j:j + 1, sl] * ext_ref[pl.ds(HALO - DN_CONV // 2 + j, ts), :]
        x = acc * _sigmoid(acc)
        x = jnp.where(valid_main, x, 0.0)
        if cb < 2 * DN_HEADS:
            ss = _lane_sum(x * x, ones_bf)
            x = x * lax.rsqrt(ss + EPS)
        if cb < DN_HEADS:
            q_ref[0, :, sl] = (x * (LANE ** -0.5)).astype(BF16)
        elif cb < 2 * DN_HEADS:
            k_ref[0, :, slice((cb - DN_HEADS) * LANE, (cb - DN_HEADS + 1) * LANE)] = x.astype(BF16)
        else:
            v_ref[0, :, slice((cb - 2 * DN_HEADS) * LANE, (cb - 2 * DN_HEADS + 1) * LANE)] = x.astype(BF16)

    ba = ba_ref[0]
    lane = lax.broadcasted_iota(jnp.int32, (ts, LANE), 1)
    beta = jnp.where(valid_main, _sigmoid(ba), 0.0)
    xs = ba + gp_ref[1:2, :]
    softplus = jnp.maximum(xs, 0.0) + jnp.log1p(jnp.exp(-jnp.abs(xs)))
    g = jnp.where(valid_main, gp_ref[0:1, :] * softplus, 0.0)
    g3 = _split3(g)
    gcf = _dot(tri_ref[0], g3[0]) + _dot(tri_ref[0], g3[1]) + _dot(tri_ref[0], g3[2])
    gcb = _dot(tri_ref[1], g3[0]) + _dot(tri_ref[1], g3[1]) + _dot(tri_ref[1], g3[2])
    gt_ref[0] = jnp.where(lane < 2 * DN_HEADS, beta,
                          jnp.where(lane < 3 * DN_HEADS, gcf, jnp.where(lane < 4 * DN_HEADS, gcb, 0.0)))


def _prep_constants(ts):
    t = np.arange(ts)[:, None]
    j = np.arange(ts + 2 * HALO)[None, :] - HALO
    bands = []
    for w in POOL_WINDOWS:
        bands.append(((j >= t - w // 2) & (j <= t + (w - 1 - w // 2))).astype(np.float32))
    band = jnp.asarray(np.stack(bands), BF16)
    a = np.arange(ts)
    same = (a[:, None] // CHUNK) == (a[None, :] // CHUNK)
    tri = np.stack([same & (a[None, :] <= a[:, None]), same & (a[None, :] >= a[:, None])]).astype(np.float32)
    return band, jnp.asarray(tri, BF16)


def _prep(up, uq, ba, pool_w, pool_scale, conv_w, gate_p, ts):
    b, lp, _ = up.shape
    nt = lp // ts
    hb = ts // HALO
    band, tri = _prep_constants(ts)
    cqkv = 2 * DN_QK + DN_V
    main = lambda n: pl.BlockSpec((1, ts, n), lambda bi, i: (bi, i, 0))
    left = lambda n: pl.BlockSpec((1, HALO, n), lambda bi, i: (bi, jnp.maximum(i * hb - 1, 0), 0))
    right = lambda n: pl.BlockSpec((1, HALO, n), lambda bi, i: (bi, jnp.minimum((i + 1) * hb, lp // HALO - 1), 0))
    out = lambda n, dt: jax.ShapeDtypeStruct((b, lp, n), dt)
    return pl.pallas_call(
        functools.partial(_prep_kernel, ts=ts, lp=lp),
        grid=(b, nt),
        in_specs=[main(POOL_WIDTH), left(POOL_WIDTH), right(POOL_WIDTH), main(cqkv), left(cqkv), right(cqkv),
                  main(LANE), _const_spec(band.shape), _const_spec(tri.shape), _const_spec(pool_w.shape),
                  _const_spec(pool_scale.shape), _const_spec(conv_w.shape), _const_spec(gate_p.shape)],
        out_specs=[main(POOL_WIDTH), main(DN_QK), main(DN_QK), main(DN_V), main(LANE)],
        out_shape=[out(POOL_WIDTH, BF16), out(DN_QK, BF16), out(DN_QK, BF16), out(DN_V, BF16), out(LANE, F32)],
        scratch_shapes=[pltpu.VMEM((ts + 2 * HALO, LANE), F32)],
        compiler_params=_params(("parallel", "parallel")),
        name="mixer_prep",
    )(up, up, up, uq, uq, uq, ba, band, tri, pool_w, pool_scale, conv_w, gate_p)


def _pair_masks():
    r = lax.broadcasted_iota(jnp.int32, (PAIR, PAIR), 0)
    c = lax.broadcasted_iota(jnp.int32, (PAIR, PAIR), 1)
    top = (r < CHUNK) & (c < CHUNK)
    bot = (r >= CHUNK) & (c >= CHUNK)
    incl = (top & (r >= c)) | (bot & (r <= c))
    strict = (top & (r > c)) | (bot & (r < c))
    m16 = (r // 16) == (c // 16)
    m32 = (r // 32) == (c // 32)
    eye = (r == c).astype(F32)
    return incl, strict, m16, (m32 & ~m16, ~m32), eye


def _unit_tri_inverse(lmats, m16, off_masks, eye):
    ns = [-jnp.where(m16, lm, 0.0) for lm in lmats]
    ts_ = [eye + n for n in ns]
    ps = ns
    for _ in range(3):
        pbs = [p.astype(BF16) for p in ps]
        ps = [_dot(pb, pb) for pb in pbs]
        ts_ = [t + _dot(t.astype(BF16), p.astype(BF16)) for t, p in zip(ts_, ps)]
    for lo_mask in off_masks:
        tbs = [t.astype(BF16) for t in ts_]
        offs = [jnp.where(lo_mask, lm, 0.0).astype(BF16) for lm in lmats]
        mids = [_dot(tb, off).astype(BF16) for tb, off in zip(tbs, offs)]
        ts_ = [t - _dot(mid, tb) for t, mid, tb in zip(ts_, mids, tbs)]
    return ts_


def _intra_chunks(q_ref, k_ref, v_ref, gt_ref, w_ref, u_ref, qd_ref, kd_ref, at_ref, gl_ref, chunks, masks):
    incl, strict, m16, off_masks, eye = masks
    lane_row = lax.broadcasted_iota(jnp.int32, (1, PAIR), 1)
    sub_col = lax.broadcasted_iota(jnp.int32, (PAIR, 1), 0)
    nh = DN_HEADS
    inst = []
    for j in chunks:
        rows = pl.ds(pl.multiple_of(j * CHUNK, CHUNK), CHUNK)
        gt = gt_ref[0, rows, :]
        g2t = jnp.concatenate([gt, gt], axis=0).T
        for h in range(nh):
            sl = slice(h * LANE, (h + 1) * LANE)
            qc, kc, vc = q_ref[0, rows, sl], k_ref[0, rows, sl], v_ref[0, rows, sl]
            k2, q2 = jnp.concatenate([kc, kc], axis=0), jnp.concatenate([qc, qc], axis=0)
            k2f = k2.astype(F32)
            bcol = jnp.concatenate([gt[:, h:h + 1], gt[:, nh + h:nh + h + 1]], axis=0)
            gcol = jnp.concatenate([gt[:, 2 * nh + h:2 * nh + h + 1], gt[:, 3 * nh + h:3 * nh + h + 1]], axis=0)
            grow = jnp.where(lane_row < CHUNK, g2t[2 * nh + h:2 * nh + h + 1, :],
                             g2t[3 * nh + h:3 * nh + h + 1, :])
            kb2 = k2f * bcol
            inst.append(dict(j=j, h=h, rows=rows, k2=k2, q2=q2, k2f=k2f, vc=vc, bcol=bcol, gcol=gcol, grow=grow,
                             kb2=kb2))
    raws = [_dot_nt(jnp.concatenate([it["kb2"].astype(BF16), it["q2"]], axis=0), it["k2"]) for it in inst]
    decays = [jnp.exp(jnp.where(incl, it["gcol"] - it["grow"], -jnp.inf)) for it in inst]
    lmats = [jnp.where(strict, raw[:PAIR] * dec, 0.0) for raw, dec in zip(raws, decays)]
    tinv = _unit_tri_inverse(lmats, m16, off_masks, eye)
    rhss = []
    for it in inst:
        e2 = jnp.exp(it["gcol"])
        v2f = jnp.concatenate([it["vc"], it["vc"]], axis=0).astype(F32)
        rhss.append(jnp.concatenate([v2f * it["bcol"], it["kb2"] * e2], axis=1).astype(BF16))
        it["e2"] = e2
    uws = [_dot(t.astype(BF16), rhs) for t, rhs in zip(tinv, rhss)]
    for it, raw, dec, uw in zip(inst, raws, decays, uws):
        h, rows, gcol = it["h"], it["rows"], it["gcol"]
        attn = raw[PAIR:] * dec
        qd2 = it["q2"].astype(F32) * it["e2"]
        gl_col = jnp.where(sub_col < CHUNK, gcol[CHUNK - 1:CHUNK, :], gcol[CHUNK:CHUNK + 1, :])
        kd2 = it["k2f"] * jnp.exp(gl_col - gcol)
        for d in range(2):
            half = slice(d * CHUNK, (d + 1) * CHUNK)
            dst = slice((d * nh + h) * LANE, (d * nh + h + 1) * LANE)
            u_ref[0, rows, dst] = uw[half, :LANE].astype(BF16)
            w_ref[0, rows, dst] = uw[half, LANE:].astype(BF16)
            qd_ref[0, rows, dst] = qd2[half].astype(BF16)
            kd_ref[0, rows, dst] = kd2[half].astype(BF16)
            at_ref[0, rows, dst] = attn[half].astype(BF16)
            g_last = gcol[CHUNK - 1:CHUNK, :] if d == 0 else gcol[CHUNK:CHUNK + 1, :]
            gl_ref[0, it["j"], d * nh + h:d * nh + h + 1, :] = jnp.broadcast_to(jnp.exp(g_last), (1, LANE))


def _delta_intra_kernel(q_ref, k_ref, v_ref, gt_ref, w_ref, u_ref, qd_ref, kd_ref, at_ref, gl_ref, *, ts):
    masks = _pair_masks()

    ncb = ts // CHUNK
    per_body = next(n for n in (5, 4, 3, 2, 1) if ncb % n == 0)

    def body(jj, _):
        _intra_chunks(q_ref, k_ref, v_ref, gt_ref, w_ref, u_ref, qd_ref, kd_ref, at_ref, gl_ref,
                      tuple(per_body * jj + i for i in range(per_body)), masks)
        return 0

    lax.fori_loop(0, ncb // per_body, body, 0)


def _delta_intra(q, k, v, gt, ts):
    b, lp, _ = q.shape
    ncb = ts // CHUNK
    wide = 2 * DN_HEADS * LANE
    tile = lambda n: pl.BlockSpec((1, ts, n), lambda bi, i: (bi, i, 0))
    big = jax.ShapeDtypeStruct((b, lp, wide), BF16)
    return pl.pallas_call(
        functools.partial(_delta_intra_kernel, ts=ts),
        grid=(b, lp // ts),
        in_specs=[tile(DN_QK), tile(DN_QK), tile(DN_V), tile(LANE)],
        out_specs=[tile(wide)] * 5 + [pl.BlockSpec((1, ncb, 2 * DN_HEADS, LANE), lambda bi, i: (bi, i, 0, 0))],
        out_shape=[big] * 5 + [jax.ShapeDtypeStruct((b, lp // CHUNK, 2 * DN_HEADS, LANE), F32)],
        compiler_params=_params(("parallel", "parallel")),
        name="delta_intra",
    )(q, k, v, gt)


def _delta_inter_kernel(wf_ref, wb_ref, uf_ref, ub_ref, qdf_ref, qdb_ref, kdf_ref, kdb_ref, atf_ref, atb_ref,
                        glf_ref, glb_ref, of_ref, ob_ref, s_ref, *, ts):
    ncb = ts // CHUNK
    nh = DN_HEADS

    @pl.when(pl.program_id(1) == 0)
    def _():
        s_ref[...] = jnp.zeros(s_ref.shape, F32)

    zero = jnp.zeros((CHUNK, LANE), BF16)
    left = lambda x: jnp.concatenate([x, zero], axis=1)
    right = lambda x: jnp.concatenate([zero, x], axis=1)

    def body(j, _):
        jb = ncb - 1 - j
        rf = pl.ds(pl.multiple_of(j * CHUNK, CHUNK), CHUNK)
        rb = pl.ds(pl.multiple_of(jb * CHUNK, CHUNK), CHUNK)
        sls = [slice(h * LANE, (h + 1) * LANE) for h in range(nh)]
        s2s = [s_ref[h] for h in range(nh)]
        lhss = [jnp.concatenate([left(wf_ref[0, rf, sl]), right(wb_ref[0, rb, sl]),
                                 left(qdf_ref[0, rf, sl]), right(qdb_ref[0, rb, sl])], axis=0) for sl in sls]
        r1s = [_dot(lhs, s2.astype(BF16)) for lhs, s2 in zip(lhss, s2s)]
        v_news = [(jnp.concatenate([uf_ref[0, rf, sl], ub_ref[0, rb, sl]], axis=0).astype(F32) - r1[:PAIR])
                  .astype(BF16) for sl, r1 in zip(sls, r1s)]
        o2s = [r1[PAIR:] + _dot(jnp.concatenate([atf_ref[0, rf, sl], atb_ref[0, rb, sl]], axis=0), v_new)
               for sl, r1, v_new in zip(sls, r1s, v_news)]
        upds = [_dot_tn(jnp.concatenate([left(kdf_ref[0, rf, sl]), right(kdb_ref[0, rb, sl])], axis=0), v_new)
                for sl, v_new in zip(sls, v_news)]
        for h in range(nh):
            of_ref[0, rf, sls[h]] = o2s[h][:CHUNK].astype(BF16)
            ob_ref[0, rb, sls[h]] = o2s[h][CHUNK:].astype(BF16)
            scale = jnp.concatenate([jnp.broadcast_to(glf_ref[0, j, h:h + 1, :], (LANE, LANE)),
                                     jnp.broadcast_to(glb_ref[0, jb, nh + h:nh + h + 1, :], (LANE, LANE))], axis=0)
            s_ref[h] = s2s[h] * scale + upds[h]
        return 0

    lax.fori_loop(0, ncb, body, 0)


def _delta_inter(w, u, qd, kd, at, gl, ts):
    b, lp, _ = w.shape
    nt = lp // ts
    ncb = ts // CHUNK
    half = DN_HEADS * LANE
    fwd = pl.BlockSpec((1, ts, half), lambda bi, i: (bi, i, 0))
    bwd = pl.BlockSpec((1, ts, half), lambda bi, i: (bi, nt - 1 - i, 1))
    glf = pl.BlockSpec((1, ncb, 2 * DN_HEADS, LANE), lambda bi, i: (bi, i, 0, 0))
    glb = pl.BlockSpec((1, ncb, 2 * DN_HEADS, LANE), lambda bi, i: (bi, nt - 1 - i, 0, 0))
    out = jax.ShapeDtypeStruct((b, lp, half), BF16)
    return pl.pallas_call(
        functools.partial(_delta_inter_kernel, ts=ts),
        grid=(b, nt),
        in_specs=[fwd, bwd] * 5 + [glf, glb],
        out_specs=[pl.BlockSpec((1, ts, half), lambda bi, i: (bi, i, 0)),
                   pl.BlockSpec((1, ts, half), lambda bi, i: (bi, nt - 1 - i, 0))],
        out_shape=[out, out],
        scratch_shapes=[pltpu.VMEM((DN_HEADS, 2 * LANE, LANE), F32)],
        compiler_params=_params(("parallel", "arbitrary")),
        name="delta_inter",
    )(w, w, u, u, qd, qd, kd, kd, at, at, gl, gl)


def _mlp_tail(h1, g_ref, w1_ref, w2_ref, o_ref):
    hn = _rms_rows(h1, g_ref[...]).astype(BF16)
    acc = jnp.zeros_like(h1)
    for c in range(D_FF // FF_CHUNK):
        sl = slice(c * FF_CHUNK, (c + 1) * FF_CHUNK)
        hid = jnp.maximum(_dot(hn, w1_ref[:, sl]), 0.0)
        acc = acc + _dot((hid * hid).astype(BF16), w2_ref[sl, :])
    o_ref[...] = h1 + acc


def _out_mlp_even_kernel(yp_ref, of_ref, ob_ref, z_ref, ng_ref, wop_ref, wod_ref, h_ref, g_ref, w1_ref, w2_ref,
                         o_ref):
    ones_bf = jnp.ones((LANE, LANE), BF16)
    h1 = h_ref[...] + _dot(yp_ref[...], wop_ref[...])
    for hh in range(DN_HEADS):
        sl = slice(hh * LANE, (hh + 1) * LANE)
        o = of_ref[:, sl].astype(F32) + ob_ref[:, sl].astype(F32)
        ms = _lane_sum(o * o, ones_bf) * (1.0 / LANE)
        zf = z_ref[:, sl].astype(F32)
        y = o * lax.rsqrt(ms + EPS) * ng_ref[...] * (zf * _sigmoid(zf))
        h1 = h1 + _dot(y.astype(BF16), wod_ref[sl, :])
    _mlp_tail(h1, g_ref, w1_ref, w2_ref, o_ref)


def _out_mlp_even(yp, of, ob, z, ng, wop, wod, h2, g, w1, w2, tt):
    r = h2.shape[0]
    row = lambda n: pl.BlockSpec((tt, n), lambda i: (i, 0))
    return pl.pallas_call(
        _out_mlp_even_kernel,
        grid=(r // tt,),
        in_specs=[row(POOL_WIDTH), row(DN_V), row(DN_V), row(DN_V), _const_spec((1, LANE)), _const_spec(wop.shape),
                  _const_spec(wod.shape), row(D_MODEL), _const_spec((1, D_MODEL)), _const_spec(w1.shape),
                  _const_spec(w2.shape)],
        out_specs=row(D_MODEL),
        out_shape=jax.ShapeDtypeStruct((r, D_MODEL), F32),
        compiler_params=_params(("parallel",)),
        name="out_proj_mlp_even",
    )(yp, of, ob, z, ng, wop, wod, h2, g, w1, w2)


def _out_mlp_odd_kernel(a_ref, wo_ref, h_ref, g_ref, w1_ref, w2_ref, o_ref):
    h1 = h_ref[...] + _dot(a_ref[...], wo_ref[...])
    _mlp_tail(h1, g_ref, w1_ref, w2_ref, o_ref)


def _out_mlp_odd(a, wo, h2, g, w1, w2, tt):
    r = h2.shape[0]
    row = lambda n: pl.BlockSpec((tt, n), lambda i: (i, 0))
    return pl.pallas_call(
        _out_mlp_odd_kernel,
        grid=(r // tt,),
        in_specs=[row(a.shape[1]), _const_spec(wo.shape), row(D_MODEL), _const_spec((1, D_MODEL)),
                  _const_spec(w1.shape), _const_spec(w2.shape)],
        out_specs=row(D_MODEL),
        out_shape=jax.ShapeDtypeStruct((r, D_MODEL), F32),
        compiler_params=_params(("parallel",)),
        name="out_proj_mlp_odd",
    )(a, wo, h2, g, w1, w2)


def _in_odd_kernel(h_ref, g_ref, wq_ref, wk_ref, wv_ref, qn_ref, kn_ref, cos_ref, sin_ref, q_ref, k_ref, v_ref):
    xn = _rms_rows(h_ref[...], g_ref[...]).astype(BF16)
    ones_bf = jnp.ones((LANE, LANE), BF16)
    cos = cos_ref[...]
    sin = sin_ref[...]

    def norm_rope(x, gain, scale):
        ms = _lane_sum(x * x, ones_bf) * (1.0 / ATT_HD)
        y = x * lax.rsqrt(ms + EPS) * gain
        return (y * cos + pltpu.roll(y, ATT_HD // 2, axis=1) * sin) * scale

    uq = _dot(xn, wq_ref[...])
    for hh in range(ATT_HEADS):
        sl = slice(hh * ATT_HD, (hh + 1) * ATT_HD)
        q_ref[:, sl] = norm_rope(uq[:, sl], qn_ref[...], ATT_HD ** -0.5 * LOG2E).astype(BF16)
    uk = _dot(xn, wk_ref[...])
    for hh in range(ATT_KV_HEADS):
        sl = slice(hh * ATT_HD, (hh + 1) * ATT_HD)
        k_ref[:, sl] = norm_rope(uk[:, sl], kn_ref[...], 1.0).astype(BF16)
    v_ref[...] = _dot(xn, wv_ref[...]).astype(BF16)


def _in_odd(h2, g, wq, wk, wv, qn, kn, cos, sin, tt):
    r = h2.shape[0]
    ntb = cos.shape[0] // tt
    row = lambda n: pl.BlockSpec((tt, n), lambda i: (i, 0))
    tab = pl.BlockSpec((tt, ATT_HD), lambda i: (i % ntb, 0))
    nq, nk = ATT_HEADS * ATT_HD, ATT_KV_HEADS * ATT_HD
    return pl.pallas_call(
        _in_odd_kernel,
        grid=(r // tt,),
        in_specs=[row(D_MODEL), _const_spec((1, D_MODEL)), _const_spec(wq.shape), _const_spec(wk.shape),
                  _const_spec(wv.shape), _const_spec((1, ATT_HD)), _const_spec((1, ATT_HD)), tab, tab],
        out_specs=[row(nq), row(nk), row(nk)],
        out_shape=[jax.ShapeDtypeStruct((r, nq), BF16), jax.ShapeDtypeStruct((r, nk), BF16),
                   jax.ShapeDtypeStruct((r, nk), BF16)],
        compiler_params=_params(("parallel",)),
        name="in_proj_odd",
    )(h2, g, wq, wk, wv, qn, kn, cos, sin)


def _attn_kernel(q_ref, k_ref, v_ref, o_ref, vaug_ref, kmax_ref, qs_ref, b_ref, acc_ref, *, tqb, tk, lp):
    sub = LANE
    n_sub = tqb // sub
    n_kc = lp // tk
    rows_q = ATT_GROUP * sub
    ones_bf = jnp.ones((LANE, LANE), BF16)

    @pl.when(pl.program_id(2) == 0)
    def _():
        def fill(c, kmax):
            rows = pl.ds(pl.multiple_of(c * tk, tk), tk)
            valid = (lax.broadcasted_iota(jnp.int32, (tk, LANE), 0) + c * tk) >= FRONT
            vaug_ref[rows, :ATT_HD] = jnp.where(valid, v_ref[0, rows, :], jnp.zeros((tk, ATT_HD), BF16))
            vaug_ref[rows, ATT_HD:] = jnp.where(valid, 1.0, 0.0).astype(BF16)
            kf = k_ref[0, rows, :].astype(F32)
            n2 = jnp.where(valid, _lane_sum(kf * kf, ones_bf), 0.0)
            return jnp.maximum(kmax, jnp.max(n2, axis=0, keepdims=True))

        kmax = lax.fori_loop(0, n_kc, fill, jnp.zeros((1, LANE), F32))
        kmax_ref[...] = jnp.broadcast_to(kmax, kmax_ref.shape)

    kmax2 = kmax_ref[0:1, :]
    for si in range(n_sub):
        q4 = q_ref[0, si * sub:(si + 1) * sub, :]
        qs = jnp.concatenate([q4[:, g * ATT_HD:(g + 1) * ATT_HD] for g in range(ATT_GROUP)], axis=0)
        qs_ref[si] = qs
        qf = qs.astype(F32)
        b_ref[si] = jnp.sqrt(_lane_sum(qf * qf, ones_bf) * kmax2) * (1.0 + 2.0 ** -10)
        acc_ref[si] = jnp.zeros((rows_q, 2 * ATT_HD), F32)

    def fast(c, _):
        rows = pl.ds(pl.multiple_of(c * tk, tk), tk)
        kc = k_ref[0, rows, :]
        va = vaug_ref[rows, :]
        for si in range(n_sub):
            s = _dot_nt(qs_ref[si], kc)
            p = jnp.exp2(s - pltpu.repeat(b_ref[si], tk // LANE, axis=1)).astype(BF16)
            acc_ref[si] += _dot(p, va)
        return 0

    lax.fori_loop(0, n_kc, fast, 0)

    def write(si, acc):
        out = (acc[:, :ATT_HD] / acc[:, ATT_HD:]).astype(BF16)
        start = si * sub if isinstance(si, int) else pl.multiple_of(si * sub, sub)
        for g in range(ATT_GROUP):
            o_ref[0, pl.ds(start, sub), g * ATT_HD:(g + 1) * ATT_HD] = out[g * sub:(g + 1) * sub]

    l_min = None
    for si in range(n_sub):
        acc = acc_ref[si]
        write(si, acc)
        m = jnp.min(acc[:, ATT_HD:])
        l_min = m if l_min is None else jnp.minimum(l_min, m)

    @pl.when(jnp.logical_not(l_min >= SOFTMAX_MIN_DENOM))
    def _():
        col_valid = lax.broadcasted_iota(jnp.int32, (rows_q, tk), 1) >= FRONT

        def redo(si, _):
            qs = qs_ref[si]

            def step(c, carry, masked):
                m, acc = carry
                rows = pl.ds(pl.multiple_of(c * tk, tk), tk)
                s = _dot_nt(qs, k_ref[0, rows, :])
                if masked:
                    s = jnp.where(col_valid, s, -jnp.inf)
                m_new = jnp.maximum(m, jnp.max(s, axis=1, keepdims=True))
                p = jnp.exp2(s - m_new).astype(BF16)
                return m_new, jnp.exp2(m - m_new) * acc + _dot(p, vaug_ref[rows, :])

            init = (jnp.full((rows_q, 1), -jnp.inf, F32), jnp.zeros((rows_q, 2 * ATT_HD), F32))
            carry = step(0, init, True)
            _, acc = lax.fori_loop(1, n_kc, lambda c, cr: step(c, cr, False), carry)
            write(si, acc)
            return 0

        lax.fori_loop(0, n_sub, redo, 0)


def _attention(q, k, v, tqb):
    b, lp, _ = q.shape
    gw = ATT_GROUP * ATT_HD
    n_sub = tqb // LANE
    rows_q = ATT_GROUP * LANE
    qspec = pl.BlockSpec((1, tqb, gw), lambda bi, hi, i: (bi, i, hi))
    kvspec = pl.BlockSpec((1, lp, ATT_HD), lambda bi, hi, i: (bi, 0, hi))
    return pl.pallas_call(
        functools.partial(_attn_kernel, tqb=tqb, tk=tqb, lp=lp),
        grid=(b, ATT_KV_HEADS, lp // tqb),
        in_specs=[qspec, kvspec, kvspec],
        out_specs=qspec,
        out_shape=jax.ShapeDtypeStruct((b, lp, ATT_HEADS * ATT_HD), BF16),
        scratch_shapes=[pltpu.VMEM((lp, 2 * ATT_HD), BF16), pltpu.VMEM((8, LANE), F32),
                        pltpu.VMEM((n_sub, rows_q, ATT_HD), BF16), pltpu.VMEM((n_sub, rows_q, LANE), F32),
                        pltpu.VMEM((n_sub, rows_q, 2 * ATT_HD), F32)],
        compiler_params=_params(("parallel", "parallel", "arbitrary")),
        name="gqa_attention",
    )(q, k, v)


def _rope_tables(lp):
    s = lp - LANE
    pos = np.arange(s)
    freqs = jnp.asarray(ROPE_THETA, F32) ** (-(jnp.arange(ATT_HD // 4, dtype=F32) / (ATT_HD // 4)))
    row = jnp.asarray(pos // GRID_W, F32)
    colp = jnp.asarray(pos % GRID_W, F32)
    ang = jnp.concatenate([row[:, None] * freqs, colp[:, None] * freqs], axis=-1)
    ang = jnp.concatenate([jnp.zeros((LANE, ATT_HD // 2), F32), ang], axis=0)
    c, sn = jnp.cos(ang), jnp.sin(ang)
    return jnp.concatenate([c, c], axis=-1), jnp.concatenate([-sn, sn], axis=-1)


def _trunk(x, meta_tokens, mix_norm, mlp_norm, w_in_even, pool_w, pool_scale, conv_qkv, a_log, dt_bias,
           delta_norm, w_out_even, w_in_odd, q_norm, k_norm, w_out_odd, w_mlp_in, w_mlp_out):
    b, s, _ = x.shape
    lp = s + LANE
    tt = _token_tile(lp)
    r = b * lp
    h = jnp.concatenate([jnp.zeros((b, FRONT, D_MODEL), F32),
                         jnp.broadcast_to(meta_tokens[None], (b, N_META, D_MODEL)), x], axis=1)
    h2 = h.reshape(r, D_MODEL)

    o1 = POOL_WIDTH
    o2 = o1 + 2 * DN_QK + DN_V
    o3 = o2 + DN_V
    w_in = w_in_even[0]
    wb = jnp.zeros((D_MODEL, LANE), F32).at[:, :4 * DN_HEADS].set(w_in[:, o3:])
    up, uq, z, ba = _in_even(h2, mix_norm[0][None], w_in[:, :o1].astype(BF16), w_in[:, o1:o2].astype(BF16),
                             w_in[:, o2:o3].astype(BF16), wb.astype(BF16), tt)
    gate_p = jnp.zeros((8, LANE), F32)
    gate_p = gate_p.at[0, 2 * DN_HEADS:4 * DN_HEADS].set(-jnp.exp(a_log[0].reshape(-1)))
    gate_p = gate_p.at[1, 2 * DN_HEADS:4 * DN_HEADS].set(dt_bias[0].reshape(-1))
    conv_w = jnp.zeros((8, 2 * DN_QK + DN_V), F32).at[:DN_CONV].set(conv_qkv[0])
    seq = lambda a: a.reshape(b, lp, a.shape[-1])
    y_pool, q, k, v, gt = _prep(seq(up), seq(uq), seq(ba), pool_w[0].astype(BF16), pool_scale[0][None], conv_w,
                                gate_p, tt)
    w, u, qd, kd, at, gl = _delta_intra(q, k, v, gt, tt)
    o_f, o_b = _delta_inter(w, u, qd, kd, at, gl, tt)
    wo = w_out_even[0].astype(BF16)
    flat = lambda a: a.reshape(r, a.shape[-1])
    h2 = _out_mlp_even(flat(y_pool), flat(o_f), flat(o_b), z, delta_norm[0][None], wo[:POOL_WIDTH], wo[POOL_WIDTH:],
                       h2, mlp_norm[0][None], w_mlp_in[0].astype(BF16), w_mlp_out[0].astype(BF16), tt)

    nq, nk = ATT_HEADS * ATT_HD, ATT_KV_HEADS * ATT_HD
    w_in = w_in_odd[0].astype(BF16)
    cos, sin = _rope_tables(lp)
    qa, ka, va = _in_odd(h2, mix_norm[1][None], w_in[:, :nq], w_in[:, nq:nq + nk], w_in[:, nq + nk:],
                         q_norm[0][None], k_norm[0][None], cos, sin, tt)
    att = _attention(seq(qa), seq(ka), seq(va), tt)
    h2 = _out_mlp_odd(flat(att), w_out_odd[0].astype(BF16), h2, mlp_norm[1][None], w_mlp_in[1].astype(BF16),
                      w_mlp_out[1].astype(BF16), tt)
    return h2.reshape(b, lp, D_MODEL)[:, LANE:]


def kernel(x_prompt, x_sample, meta_tokens, mix_norm, mlp_norm, w_in_even, pool_w, pool_scale, conv_qkv, a_log,
           dt_bias, delta_norm, w_out_even, w_in_odd, q_norm, k_norm, w_out_odd, w_mlp_in, w_mlp_out):
    nb = x_prompt.shape[0]
    x = jnp.concatenate([x_prompt, x_sample], axis=0)
    y = _trunk(x, meta_tokens, mix_norm, mlp_norm, w_in_even, pool_w, pool_scale, conv_qkv, a_log, dt_bias,
               delta_norm, w_out_even, w_in_odd, q_norm, k_norm, w_out_odd, w_mlp_in, w_mlp_out)
    return y[:nb], y[nb:]
```

```python
import functools
import math

import jax
import jax.numpy as jnp
import numpy as np
from jax import lax
from jax.experimental import pallas as pl
from jax.experimental.pallas import tpu as pltpu

F32 = jnp.float32
BF16 = jnp.bfloat16

D_MODEL = 1024
N_META = 16
LANE = 128
MXU_TILE = 256
HALO = 16
FRONT = LANE - N_META
GRID_W = 64
EPS = 1e-6
POOL_WINDOWS = (2, 4, 8, 16)
POOL_WIDTH = 512
DN_HEADS = 4
DN_CONV = 7
CHUNK = 64
PAIR = 2 * CHUNK
DN_QK = 512
DN_V = 512
ATT_HD = 128
ATT_HEADS = 8
ATT_KV_HEADS = 2
ATT_GROUP = 4
ROPE_THETA = 10000.0
D_FF = 4096
FF_CHUNK = 1024
VMEM_LIMIT = 56 * 1024 * 1024
LOG2E = math.log2(math.e)
SOFTMAX_MIN_DENOM = 1e-26


def _token_tile(lp):
    for t in (640, 512, 384, 256, 128):
        if lp % t == 0:
            return t
    raise ValueError(lp)


def _params(sem):
    return pltpu.CompilerParams(dimension_semantics=sem, vmem_limit_bytes=VMEM_LIMIT)


def _const_spec(shape):
    nd = len(shape)
    return pl.BlockSpec(shape, lambda *_: (0,) * nd, pipeline_mode=pl.Buffered(1))


def _dot(a, b):
    return jnp.dot(a, b, preferred_element_type=F32)


def _dot_nt(a, b):
    return lax.dot_general(a, b, (((1,), (1,)), ((), ())), preferred_element_type=F32)


def _dot_tn(a, b):
    return lax.dot_general(a, b, (((0,), (0,)), ((), ())), preferred_element_type=F32)


def _split2(x):
    hi = x.astype(BF16)
    lo = (x - hi.astype(F32)).astype(BF16)
    return hi, lo


def _split3(x):
    hi = x.astype(BF16)
    r = x - hi.astype(F32)
    mid = r.astype(BF16)
    lo = (r - mid.astype(F32)).astype(BF16)
    return hi, mid, lo


def _lane_sum(x, ones_bf):
    hi, lo = _split2(x)
    return _dot(hi, ones_bf) + _dot(lo, ones_bf)


def _sigmoid(x):
    return 1.0 / (1.0 + jnp.exp(-x))


def _rms_rows(x, g):
    ms = jnp.mean(x * x, axis=-1, keepdims=True)
    return x * lax.rsqrt(ms + EPS) * g


def _in_even_kernel(h_ref, g_ref, wp_ref, wq_ref, wz_ref, wb_ref, up_ref, uq_ref, z_ref, ba_ref):
    xn = _rms_rows(h_ref[...], g_ref[...]).astype(BF16)
    up_ref[...] = _dot(xn, wp_ref[...]).astype(BF16)
    uq_ref[...] = _dot(xn, wq_ref[...]).astype(BF16)
    z_ref[...] = _dot(xn, wz_ref[...]).astype(BF16)
    ba_ref[...] = _dot(xn, wb_ref[...])


def _in_even(h2, g, wp, wq, wz, wb, tt):
    r = h2.shape[0]
    row = lambda n: pl.BlockSpec((tt, n), lambda i: (i, 0))
    return pl.pallas_call(
        _in_even_kernel,
        grid=(r // tt,),
        in_specs=[row(D_MODEL), _const_spec((1, D_MODEL)), _const_spec(wp.shape), _const_spec(wq.shape),
                  _const_spec(wz.shape), _const_spec(wb.shape)],
        out_specs=[row(POOL_WIDTH), row(2 * DN_QK + DN_V), row(DN_V), row(LANE)],
        out_shape=[jax.ShapeDtypeStruct((r, POOL_WIDTH), BF16), jax.ShapeDtypeStruct((r, 2 * DN_QK + DN_V), BF16),
                   jax.ShapeDtypeStruct((r, DN_V), BF16), jax.ShapeDtypeStruct((r, LANE), F32)],
        compiler_params=_params(("parallel",)),
        name="in_proj_even",
    )(h2, g, wp, wq, wz, wb)


def _prep_kernel(up_ref, upl_ref, upr_ref, uq_ref, uql_ref, uqr_ref, ba_ref, band_ref, tri_ref, pw_ref, ps_ref,
                 cw_ref, gp_ref, yp_ref, q_ref, k_ref, v_ref, gt_ref, ext_ref, *, ts, lp):
    i = pl.program_id(1)
    row0 = i * ts
    rows_ext = lax.broadcasted_iota(jnp.int32, (ts + 2 * HALO, LANE), 0) + (row0 - HALO)
    valid_ext = (rows_ext >= FRONT) & (rows_ext < lp)
    valid_main = valid_ext[HALO:HALO + ts]
    ones_bf = jnp.ones((LANE, LANE), BF16)

    validb = jnp.where(valid_ext, 1.0, 0.0).astype(BF16)
    for gi in range(len(POOL_WINDOWS)):
        sl = slice(gi * LANE, (gi + 1) * LANE)
        eg = jnp.concatenate([upl_ref[0, :, sl], up_ref[0, :, sl], upr_ref[0, :, sl]], axis=0)
        eg = jnp.where(valid_ext, eg, jnp.zeros_like(eg))
        sc = _dot(band_ref[gi], jnp.concatenate([eg, validb], axis=1))
        mean = sc[:, :LANE] / jnp.maximum(sc[:, LANE:], 1.0)
        d = (mean - up_ref[0, :, sl].astype(F32)).astype(BF16)
        y = _dot(d, pw_ref[gi]) * ps_ref[:, sl]
        yp_ref[0, :, sl] = jnp.where(valid_main, y, 0.0).astype(BF16)

    n_blk = (2 * DN_QK + DN_V) // LANE
    for cb in range(n_blk):
        sl = slice(cb * LANE, (cb + 1) * LANE)
        e = jnp.concatenate([uql_ref[0, :, sl], uq_ref[0, :, sl], uqr_ref[0, :, sl]], axis=0).astype(F32)
        ext_ref[...] = jnp.where(valid_ext, e, 0.0)
        acc = jnp.zeros((ts, LANE), F32)
        for j in range(DN_CONV):
            acc = acc + cw_ref[j:j + 1, sl] * ext_ref[pl.ds(HALO - DN_CONV // 2 + j, ts), :]
        x = acc * _sigmoid(acc)
        x = jnp.where(valid_main, x, 0.0)
        if cb < 2 * DN_HEADS:
            ss = _lane_sum(x * x, ones_bf)
            x = x * lax.rsqrt(ss + EPS)
        if cb < DN_HEADS:
            q_ref[0, :, sl] = (x * (LANE ** -0.5)).astype(BF16)
        elif cb < 2 * DN_HEADS:
            k_ref[0, :, slice((cb - DN_HEADS) * LANE, (cb - DN_HEADS + 1) * LANE)] = x.astype(BF16)
        else:
            v_ref[0, :, slice((cb - 2 * DN_HEADS) * LANE, (cb - 2 * DN_HEADS + 1) * LANE)] = x.astype(BF16)

    ba = ba_ref[0]
    lane = lax.broadcasted_iota(jnp.int32, (ts, LANE), 1)
    beta = jnp.where(valid_main, _sigmoid(ba), 0.0)
    xs = ba + gp_ref[1:2, :]
    softplus = jnp.maximum(xs, 0.0) + jnp.log1p(jnp.exp(-jnp.abs(xs)))
    g = jnp.where(valid_main, gp_ref[0:1, :] * softplus, 0.0)
    g3 = _split3(g)
    gcf = _dot(tri_ref[0], g3[0]) + _dot(tri_ref[0], g3[1]) + _dot(tri_ref[0], g3[2])
    gcb = _dot(tri_ref[1], g3[0]) + _dot(tri_ref[1], g3[1]) + _dot(tri_ref[1], g3[2])
    gt_ref[0] = jnp.where(lane < 2 * DN_HEADS, beta,
                          jnp.where(lane < 3 * DN_HEADS, gcf, jnp.where(lane < 4 * DN_HEADS, gcb, 0.0)))


def _prep_constants(ts):
    t = np.arange(ts)[:, None]
    j = np.arange(ts + 2 * HALO)[None, :] - HALO
    bands = []
    for w in POOL_WINDOWS:
        bands.append(((j >= t - w // 2) & (j <= t + (w - 1 - w // 2))).astype(np.float32))
    band = jnp.asarray(np.stack(bands), BF16)
    a = np.arange(ts)
    same = (a[:, None] // CHUNK) == (a[None, :] // CHUNK)
    tri = np.stack([same & (a[None, :] <= a[:, None]), same & (a[None, :] >= a[:, None])]).astype(np.float32)
    return band, jnp.asarray(tri, BF16)


def _prep(up, uq, ba, pool_w, pool_scale, conv_w, gate_p, ts):
    b, lp, _ = up.shape
    nt = lp // ts
    hb = ts // HALO
    band, tri = _prep_constants(ts)
    cqkv = 2 * DN_QK + DN_V
    main = lambda n: pl.BlockSpec((1, ts, n), lambda bi, i: (bi, i, 0))
    left = lambda n: pl.BlockSpec((1, HALO, n), lambda bi, i: (bi, jnp.maximum(i * hb - 1, 0), 0))
    right = lambda n: pl.BlockSpec((1, HALO, n), lambda bi, i: (bi, jnp.minimum((i + 1) * hb, lp // HALO - 1), 0))
    out = lambda n, dt: jax.ShapeDtypeStruct((b, lp, n), dt)
    return pl.pallas_call(
        functools.partial(_prep_kernel, ts=ts, lp=lp),
        grid=(b, nt),
        in_specs=[main(POOL_WIDTH), left(POOL_WIDTH), right(POOL_WIDTH), main(cqkv), left(cqkv), right(cqkv),
                  main(LANE), _const_spec(band.shape), _const_spec(tri.shape), _const_spec(pool_w.shape),
                  _const_spec(pool_scale.shape), _const_spec(conv_w.shape), _const_spec(gate_p.shape)],
        out_specs=[main(POOL_WIDTH), main(DN_QK), main(DN_QK), main(DN_V), main(LANE)],
        out_shape=[out(POOL_WIDTH, BF16), out(DN_QK, BF16), out(DN_QK, BF16), out(DN_V, BF16), out(LANE, F32)],
        scratch_shapes=[pltpu.VMEM((ts + 2 * HALO, LANE), F32)],
        compiler_params=_params(("parallel", "parallel")),
        name="mixer_prep",
    )(up, up, up, uq, uq, uq, ba, band, tri, pool_w, pool_scale, conv_w, gate_p)


def _pair_masks():
    r = lax.broadcasted_iota(jnp.int32, (PAIR, PAIR), 0)
    c = lax.broadcasted_iota(jnp.int32, (PAIR, PAIR), 1)
    top = (r < CHUNK) & (c < CHUNK)
    bot = (r >= CHUNK) & (c >= CHUNK)
    incl = (top & (r >= c)) | (bot & (r <= c))
    strict = (top & (r > c)) | (bot & (r < c))
    m16 = (r // 16) == (c // 16)
    m32 = (r // 32) == (c // 32)
    eye = (r == c).astype(F32)
    return incl, strict, m16, (m32 & ~m16, ~m32), eye


def _unit_tri_inverse(lmats, m16, off_masks, eye):
    ns = [-jnp.where(m16, lm, 0.0) for lm in lmats]
    ts_ = [eye + n for n in ns]
    ps = ns
    for _ in range(3):
        pbs = [p.astype(BF16) for p in ps]
        ps = [_dot(pb, pb) for pb in pbs]
        ts_ = [t + _dot(t.astype(BF16), p.astype(BF16)) for t, p in zip(ts_, ps)]
    for lo_mask in off_masks:
        tbs = [t.astype(BF16) for t in ts_]
        offs = [jnp.where(lo_mask, lm, 0.0).astype(BF16) for lm in lmats]
        mids = [_dot(tb, off).astype(BF16) for tb, off in zip(tbs, offs)]
        ts_ = [t - _dot(mid, tb) for t, mid, tb in zip(ts_, mids, tbs)]
    return ts_


def _intra_chunks(q_ref, k_ref, v_ref, gt_ref, w_ref, u_ref, qd_ref, kd_ref, at_ref, gl_ref, chunks, masks):
    incl, strict, m16, off_masks, eye = masks
    lane_row = lax.broadcasted_iota(jnp.int32, (1, PAIR), 1)
    sub_col = lax.broadcasted_iota(jnp.int32, (PAIR, 1), 0)
    nh = DN_HEADS
    inst = []
    for j in chunks:
        rows = pl.ds(pl.multiple_of(j * CHUNK, CHUNK), CHUNK)
        gt = gt_ref[0, rows, :]
        g2t = jnp.concatenate([gt, gt], axis=0).T
        for h in range(nh):
            sl = slice(h * LANE, (h + 1) * LANE)
            qc, kc, vc = q_ref[0, rows, sl], k_ref[0, rows, sl], v_ref[0, rows, sl]
            k2, q2 = jnp.concatenate([kc, kc], axis=0), jnp.concatenate([qc, qc], axis=0)
            k2f = k2.astype(F32)
            bcol = jnp.concatenate([gt[:, h:h + 1], gt[:, nh + h:nh + h + 1]], axis=0)
            gcol = jnp.concatenate([gt[:, 2 * nh + h:2 * nh + h + 1], gt[:, 3 * nh + h:3 * nh + h + 1]], axis=0)
            grow = jnp.where(lane_row < CHUNK, g2t[2 * nh + h:2 * nh + h + 1, :],
                             g2t[3 * nh + h:3 * nh + h + 1, :])
            kb2 = k2f * bcol
            inst.append(dict(j=j, h=h, rows=rows, k2=k2, q2=q2, k2f=k2f, vc=vc, bcol=bcol, gcol=gcol, grow=grow,
                             kb2=kb2))
    raws = [_dot_nt(jnp.concatenate([it["kb2"].astype(BF16), it["q2"]], axis=0), it["k2"]) for it in inst]
    decays = [jnp.exp(jnp.where(incl, it["gcol"] - it["grow"], -jnp.inf)) for it in inst]
    lmats = [jnp.where(strict, raw[:PAIR] * dec, 0.0) for raw, dec in zip(raws, decays)]
    tinv = _unit_tri_inverse(lmats, m16, off_masks, eye)
    rhss = []
    for it in inst:
        e2 = jnp.exp(it["gcol"])
        v2f = jnp.concatenate([it["vc"], it["vc"]], axis=0).astype(F32)
        rhss.append(jnp.concatenate([v2f * it["bcol"], it["kb2"] * e2], axis=1).astype(BF16))
        it["e2"] = e2
    uws = [_dot(t.astype(BF16), rhs) for t, rhs in zip(tinv, rhss)]
    for it, raw, dec, uw in zip(inst, raws, decays, uws):
        h, rows, gcol = it["h"], it["rows"], it["gcol"]
        attn = raw[PAIR:] * dec
        qd2 = it["q2"].astype(F32) * it["e2"]
        gl_col = jnp.where(sub_col < CHUNK, gcol[CHUNK - 1:CHUNK, :], gcol[CHUNK:CHUNK + 1, :])
        kd2 = it["k2f"] * jnp.exp(gl_col - gcol)
        for d in range(2):
            half = slice(d * CHUNK, (d + 1) * CHUNK)
            dst = slice((d * nh + h) * LANE, (d * nh + h + 1) * LANE)
            u_ref[0, rows, dst] = uw[half, :LANE].astype(BF16)
            w_ref[0, rows, dst] = uw[half, LANE:].astype(BF16)
            qd_ref[0, rows, dst] = qd2[half].astype(BF16)
            kd_ref[0, rows, dst] = kd2[half].astype(BF16)
            at_ref[0, rows, dst] = attn[half].astype(BF16)
            g_last = gcol[CHUNK - 1:CHUNK, :] if d == 0 else gcol[CHUNK:CHUNK + 1, :]
            gl_ref[0, it["j"], d * nh + h:d * nh + h + 1, :] = jnp.broadcast_to(jnp.exp(g_last), (1, LANE))


def _delta_intra_kernel(q_ref, k_ref, v_ref, gt_ref, w_ref, u_ref, qd_ref, kd_ref, at_ref, gl_ref, *, ts):
    masks = _pair_masks()

    ncb = ts // CHUNK
    per_body = next(n for n in (5, 4, 3, 2, 1) if ncb % n == 0)

    def body(jj, _):
        _intra_chunks(q_ref, k_ref, v_ref, gt_ref, w_ref, u_ref, qd_ref, kd_ref, at_ref, gl_ref,
                      tuple(per_body * jj + i for i in range(per_body)), masks)
        return 0

    lax.fori_loop(0, ncb // per_body, body, 0)


def _delta_intra(q, k, v, gt, ts):
    b, lp, _ = q.shape
    ncb = ts // CHUNK
    wide = 2 * DN_HEADS * LANE
    tile = lambda n: pl.BlockSpec((1, ts, n), lambda bi, i: (bi, i, 0))
    big = jax.ShapeDtypeStruct((b, lp, wide), BF16)
    return pl.pallas_call(
        functools.partial(_delta_intra_kernel, ts=ts),
        grid=(b, lp // ts),
        in_specs=[tile(DN_QK), tile(DN_QK), tile(DN_V), tile(LANE)],
        out_specs=[tile(wide)] * 5 + [pl.BlockSpec((1, ncb, 2 * DN_HEADS, LANE), lambda bi, i: (bi, i, 0, 0))],
        out_shape=[big] * 5 + [jax.ShapeDtypeStruct((b, lp // CHUNK, 2 * DN_HEADS, LANE), F32)],
        compiler_params=_params(("parallel", "parallel")),
        name="delta_intra",
    )(q, k, v, gt)


def _delta_inter_kernel(wf_ref, wb_ref, uf_ref, ub_ref, qdf_ref, qdb_ref, kdf_ref, kdb_ref, atf_ref, atb_ref,
                        glf_ref, glb_ref, of_ref, ob_ref, s_ref, *, ts, nb):
    ncb = ts // CHUNK
    nh = DN_HEADS

    @pl.when(pl.program_id(1) == 0)
    def _():
        s_ref[...] = jnp.zeros(s_ref.shape, F32)

    zero = jnp.zeros((CHUNK, LANE), BF16)
    left = lambda x: jnp.concatenate([x, zero], axis=1)
    right = lambda x: jnp.concatenate([zero, x], axis=1)

    chains = [(bb, h, slice(h * LANE, (h + 1) * LANE)) for bb in range(nb) for h in range(nh)]

    def body(j, _):
        jb = ncb - 1 - j
        rf = pl.ds(pl.multiple_of(j * CHUNK, CHUNK), CHUNK)
        rb = pl.ds(pl.multiple_of(jb * CHUNK, CHUNK), CHUNK)
        s2s = [s_ref[bb * nh + h] for bb, h, _ in chains]
        lhss = [jnp.concatenate([left(wf_ref[bb, rf, sl]), right(wb_ref[bb, rb, sl]),
                                 left(qdf_ref[bb, rf, sl]), right(qdb_ref[bb, rb, sl])], axis=0)
                for bb, _, sl in chains]
        r1s = [_dot(lhs, s2.astype(BF16)) for lhs, s2 in zip(lhss, s2s)]
        v_news = [(jnp.concatenate([uf_ref[bb, rf, sl], ub_ref[bb, rb, sl]], axis=0).astype(F32) - r1[:PAIR])
                  .astype(BF16) for (bb, _, sl), r1 in zip(chains, r1s)]
        o2s = [r1[PAIR:] + _dot(jnp.concatenate([atf_ref[bb, rf, sl], atb_ref[bb, rb, sl]], axis=0), v_new)
               for (bb, _, sl), r1, v_new in zip(chains, r1s, v_news)]
        upds = [_dot_tn(jnp.concatenate([left(kdf_ref[bb, rf, sl]), right(kdb_ref[bb, rb, sl])], axis=0), v_new)
                for (bb, _, sl), v_new in zip(chains, v_news)]
        for (bb, h, sl), s2, o2, upd in zip(chains, s2s, o2s, upds):
            of_ref[bb, rf, sl] = o2[:CHUNK].astype(BF16)
            ob_ref[bb, rb, sl] = o2[CHUNK:].astype(BF16)
            scale = jnp.concatenate([jnp.broadcast_to(glf_ref[bb, j, h:h + 1, :], (LANE, LANE)),
                                     jnp.broadcast_to(glb_ref[bb, jb, nh + h:nh + h + 1, :], (LANE, LANE))], axis=0)
            s_ref[bb * nh + h] = s2 * scale + upd
        return 0

    lax.fori_loop(0, ncb, body, 0)


def _delta_inter(w, u, qd, kd, at, gl, ts):
    b, lp, _ = w.shape
    nt = lp // ts
    ncb = ts // CHUNK
    half = DN_HEADS * LANE
    nb = 2 if b % 2 == 0 else 1
    fwd = pl.BlockSpec((nb, ts, half), lambda bi, i: (bi, i, 0))
    bwd = pl.BlockSpec((nb, ts, half), lambda bi, i: (bi, nt - 1 - i, 1))
    glf = pl.BlockSpec((nb, ncb, 2 * DN_HEADS, LANE), lambda bi, i: (bi, i, 0, 0))
    glb = pl.BlockSpec((nb, ncb, 2 * DN_HEADS, LANE), lambda bi, i: (bi, nt - 1 - i, 0, 0))
    out = jax.ShapeDtypeStruct((b, lp, half), BF16)
    return pl.pallas_call(
        functools.partial(_delta_inter_kernel, ts=ts, nb=nb),
        grid=(b // nb, nt),
        in_specs=[fwd, bwd] * 5 + [glf, glb],
        out_specs=[pl.BlockSpec((nb, ts, half), lambda bi, i: (bi, i, 0)),
                   pl.BlockSpec((nb, ts, half), lambda bi, i: (bi, nt - 1 - i, 0))],
        out_shape=[out, out],
        scratch_shapes=[pltpu.VMEM((nb * DN_HEADS, 2 * LANE, LANE), F32)],
        compiler_params=_params(("parallel", "arbitrary")),
        name="delta_inter",
    )(w, w, u, u, qd, qd, kd, kd, at, at, gl, gl)


def _mlp_tail(h1, g_ref, w1_ref, w2_ref, o_ref):
    hn = _rms_rows(h1, g_ref[...]).astype(BF16)
    acc = jnp.zeros_like(h1)
    for c in range(D_FF // FF_CHUNK):
        sl = slice(c * FF_CHUNK, (c + 1) * FF_CHUNK)
        hid = jnp.maximum(_dot(hn, w1_ref[:, sl]), 0.0)
        acc = acc + _dot((hid * hid).astype(BF16), w2_ref[sl, :])
    o_ref[...] = h1 + acc


def _out_mlp_even_kernel(yp_ref, of_ref, ob_ref, z_ref, ng_ref, wop_ref, wod_ref, h_ref, g_ref, w1_ref, w2_ref,
                         o_ref):
    ones_bf = jnp.ones((LANE, LANE), BF16)
    h1 = h_ref[...] + _dot(yp_ref[...], wop_ref[...])
    for hh in range(DN_HEADS):
        sl = slice(hh * LANE, (hh + 1) * LANE)
        o = of_ref[:, sl].astype(F32) + ob_ref[:, sl].astype(F32)
        ms = _lane_sum(o * o, ones_bf) * (1.0 / LANE)
        zf = z_ref[:, sl].astype(F32)
        y = o * lax.rsqrt(ms + EPS) * ng_ref[...] * (zf * _sigmoid(zf))
        h1 = h1 + _dot(y.astype(BF16), wod_ref[sl, :])
    _mlp_tail(h1, g_ref, w1_ref, w2_ref, o_ref)


def _out_mlp_even(yp, of, ob, z, ng, wop, wod, h2, g, w1, w2, tt):
    r = h2.shape[0]
    row = lambda n: pl.BlockSpec((tt, n), lambda i: (i, 0))
    return pl.pallas_call(
        _out_mlp_even_kernel,
        grid=(r // tt,),
        in_specs=[row(POOL_WIDTH), row(DN_V), row(DN_V), row(DN_V), _const_spec((1, LANE)), _const_spec(wop.shape),
                  _const_spec(wod.shape), row(D_MODEL), _const_spec((1, D_MODEL)), _const_spec(w1.shape),
                  _const_spec(w2.shape)],
        out_specs=row(D_MODEL),
        out_shape=jax.ShapeDtypeStruct((r, D_MODEL), F32),
        compiler_params=_params(("parallel",)),
        name="out_proj_mlp_even",
    )(yp, of, ob, z, ng, wop, wod, h2, g, w1, w2)


def _out_mlp_odd_kernel(a_ref, wo_ref, h_ref, g_ref, w1_ref, w2_ref, o_ref):
    h1 = h_ref[...] + _dot(a_ref[...], wo_ref[...])
    _mlp_tail(h1, g_ref, w1_ref, w2_ref, o_ref)


def _out_mlp_odd(a, wo, h2, g, w1, w2, tt):
    r = h2.shape[0]
    row = lambda n: pl.BlockSpec((tt, n), lambda i: (i, 0))
    return pl.pallas_call(
        _out_mlp_odd_kernel,
        grid=(r // tt,),
        in_specs=[row(a.shape[1]), _const_spec(wo.shape), row(D_MODEL), _const_spec((1, D_MODEL)),
                  _const_spec(w1.shape), _const_spec(w2.shape)],
        out_specs=row(D_MODEL),
        out_shape=jax.ShapeDtypeStruct((r, D_MODEL), F32),
        compiler_params=_params(("parallel",)),
        name="out_proj_mlp_odd",
    )(a, wo, h2, g, w1, w2)


def _in_odd_kernel(h_ref, g_ref, wq_ref, wk_ref, wv_ref, qn_ref, kn_ref, cos_ref, sin_ref, q_ref, k_ref, v_ref):
    xn = _rms_rows(h_ref[...], g_ref[...]).astype(BF16)
    ones_bf = jnp.ones((LANE, LANE), BF16)
    cos = cos_ref[...]
    sin = sin_ref[...]

    def norm_rope(x, gain, scale):
        ms = _lane_sum(x * x, ones_bf) * (1.0 / ATT_HD)
        y = x * lax.rsqrt(ms + EPS) * gain
        return (y * cos + pltpu.roll(y, ATT_HD // 2, axis=1) * sin) * scale

    uq = _dot(xn, wq_ref[...])
    for hh in range(ATT_HEADS):
        sl = slice(hh * ATT_HD, (hh + 1) * ATT_HD)
        q_ref[:, sl] = norm_rope(uq[:, sl], qn_ref[...], ATT_HD ** -0.5 * LOG2E).astype(BF16)
    uk = _dot(xn, wk_ref[...])
    for hh in range(ATT_KV_HEADS):
        sl = slice(hh * ATT_HD, (hh + 1) * ATT_HD)
        k_ref[:, sl] = norm_rope(uk[:, sl], kn_ref[...], 1.0).astype(BF16)
    v_ref[...] = _dot(xn, wv_ref[...]).astype(BF16)


def _in_odd(h2, g, wq, wk, wv, qn, kn, cos, sin, tt):
    r = h2.shape[0]
    ntb = cos.shape[0] // tt
    row = lambda n: pl.BlockSpec((tt, n), lambda i: (i, 0))
    tab = pl.BlockSpec((tt, ATT_HD), lambda i: (i % ntb, 0))
    nq, nk = ATT_HEADS * ATT_HD, ATT_KV_HEADS * ATT_HD
    return pl.pallas_call(
        _in_odd_kernel,
        grid=(r // tt,),
        in_specs=[row(D_MODEL), _const_spec((1, D_MODEL)), _const_spec(wq.shape), _const_spec(wk.shape),
                  _const_spec(wv.shape), _const_spec((1, ATT_HD)), _const_spec((1, ATT_HD)), tab, tab],
        out_specs=[row(nq), row(nk), row(nk)],
        out_shape=[jax.ShapeDtypeStruct((r, nq), BF16), jax.ShapeDtypeStruct((r, nk), BF16),
                   jax.ShapeDtypeStruct((r, nk), BF16)],
        compiler_params=_params(("parallel",)),
        name="in_proj_odd",
    )(h2, g, wq, wk, wv, qn, kn, cos, sin)


def _attn_kernel(q_ref, k_ref, v_ref, o_ref, kp_ref, vaug_ref, kmax_ref, qs_ref, b_ref, acc_ref, *, tqb, tk, lp):
    sub = LANE
    n_sub = tqb // sub
    lpk = kp_ref.shape[0]
    n_kc = lpk // tk
    rows_q = ATT_GROUP * sub
    ones_bf = jnp.ones((LANE, LANE), BF16)

    @pl.when(pl.program_id(2) == 0)
    def _():
        def fill(c, kmax):
            rows = pl.ds(pl.multiple_of(c * tqb, tqb), tqb)
            valid = (lax.broadcasted_iota(jnp.int32, (tqb, LANE), 0) + c * tqb) >= FRONT
            vaug_ref[rows, :ATT_HD] = jnp.where(valid, v_ref[0, rows, :], jnp.zeros((tqb, ATT_HD), BF16))
            vaug_ref[rows, ATT_HD:] = jnp.where(valid, 1.0, 0.0).astype(BF16)
            kc = k_ref[0, rows, :]
            kp_ref[rows, :] = kc
            kf = kc.astype(F32)
            n2 = jnp.where(valid, _lane_sum(kf * kf, ones_bf), 0.0)
            return jnp.maximum(kmax, jnp.max(n2, axis=0, keepdims=True))

        kmax = lax.fori_loop(0, lp // tqb, fill, jnp.zeros((1, LANE), F32))
        kmax_ref[...] = jnp.broadcast_to(kmax, kmax_ref.shape)
        if lpk > lp:
            kp_ref[lp:, :] = jnp.zeros((lpk - lp, ATT_HD), BF16)
            vaug_ref[lp:, :] = jnp.zeros((lpk - lp, 2 * ATT_HD), BF16)

    kmax2 = kmax_ref[0:1, :]
    for si in range(n_sub):
        q4 = q_ref[0, si * sub:(si + 1) * sub, :]
        qs = jnp.concatenate([q4[:, g * ATT_HD:(g + 1) * ATT_HD] for g in range(ATT_GROUP)], axis=0)
        qs_ref[si] = qs
        qf = qs.astype(F32)
        b_ref[si] = jnp.sqrt(_lane_sum(qf * qf, ones_bf) * kmax2) * (1.0 + 2.0 ** -10)
        acc_ref[si] = jnp.zeros((rows_q, 2 * ATT_HD), F32)

    def fast(c, _):
        rows = pl.ds(pl.multiple_of(c * tk, tk), tk)
        kc = kp_ref[rows, :]
        va = vaug_ref[rows, :]
        for si in range(n_sub):
            s = _dot_nt(qs_ref[si], kc)
            p = jnp.exp2(s - pltpu.repeat(b_ref[si], tk // LANE, axis=1)).astype(BF16)
            acc_ref[si] += _dot(p, va)
        return 0

    lax.fori_loop(0, n_kc, fast, 0)

    def write(si, acc):
        out = (acc[:, :ATT_HD] / acc[:, ATT_HD:]).astype(BF16)
        start = si * sub if isinstance(si, int) else pl.multiple_of(si * sub, sub)
        for g in range(ATT_GROUP):
            o_ref[0, pl.ds(start, sub), g * ATT_HD:(g + 1) * ATT_HD] = out[g * sub:(g + 1) * sub]

    l_min = None
    for si in range(n_sub):
        acc = acc_ref[si]
        write(si, acc)
        m = jnp.min(acc[:, ATT_HD:])
        l_min = m if l_min is None else jnp.minimum(l_min, m)

    @pl.when(jnp.logical_not(l_min >= SOFTMAX_MIN_DENOM))
    def _():
        col = lax.broadcasted_iota(jnp.int32, (rows_q, tk), 1)

        def redo(si, _):
            qs = qs_ref[si]

            def step(c, carry):
                m, acc = carry
                rows = pl.ds(pl.multiple_of(c * tk, tk), tk)
                key = col + c * tk
                s = jnp.where((key >= FRONT) & (key < lp), _dot_nt(qs, kp_ref[rows, :]), -jnp.inf)
                m_new = jnp.maximum(m, jnp.max(s, axis=1, keepdims=True))
                p = jnp.exp2(s - m_new).astype(BF16)
                return m_new, jnp.exp2(m - m_new) * acc + _dot(p, vaug_ref[rows, :])

            init = (jnp.full((rows_q, 1), -jnp.inf, F32), jnp.zeros((rows_q, 2 * ATT_HD), F32))
            _, acc = lax.fori_loop(0, n_kc, step, init)
            write(si, acc)
            return 0

        lax.fori_loop(0, n_sub, redo, 0)


def _attention(q, k, v, tqb):
    b, lp, _ = q.shape
    gw = ATT_GROUP * ATT_HD
    n_sub = tqb // LANE
    rows_q = ATT_GROUP * LANE
    qspec = pl.BlockSpec((1, tqb, gw), lambda bi, hi, i: (bi, i, hi))
    kvspec = pl.BlockSpec((1, lp, ATT_HD), lambda bi, hi, i: (bi, 0, hi))
    tk = min(3 * MXU_TILE, -(-lp // MXU_TILE) * MXU_TILE)
    lpk = -(-lp // tk) * tk
    return pl.pallas_call(
        functools.partial(_attn_kernel, tqb=tqb, tk=tk, lp=lp),
        grid=(b, ATT_KV_HEADS, lp // tqb),
        in_specs=[qspec, kvspec, kvspec],
        out_specs=qspec,
        out_shape=jax.ShapeDtypeStruct((b, lp, ATT_HEADS * ATT_HD), BF16),
        scratch_shapes=[pltpu.VMEM((lpk, ATT_HD), BF16), pltpu.VMEM((lpk, 2 * ATT_HD), BF16),
                        pltpu.VMEM((8, LANE), F32),
                        pltpu.VMEM((n_sub, rows_q, ATT_HD), BF16), pltpu.VMEM((n_sub, rows_q, LANE), F32),
                        pltpu.VMEM((n_sub, rows_q, 2 * ATT_HD), F32)],
        compiler_params=_params(("parallel", "parallel", "arbitrary")),
        name="gqa_attention",
    )(q, k, v)


def _rope_tables(lp):
    s = lp - LANE
    pos = np.arange(s)
    freqs = jnp.asarray(ROPE_THETA, F32) ** (-(jnp.arange(ATT_HD // 4, dtype=F32) / (ATT_HD // 4)))
    row = jnp.asarray(pos // GRID_W, F32)
    colp = jnp.asarray(pos % GRID_W, F32)
    ang = jnp.concatenate([row[:, None] * freqs, colp[:, None] * freqs], axis=-1)
    ang = jnp.concatenate([jnp.zeros((LANE, ATT_HD // 2), F32), ang], axis=0)
    c, sn = jnp.cos(ang), jnp.sin(ang)
    return jnp.concatenate([c, c], axis=-1), jnp.concatenate([-sn, sn], axis=-1)


def _trunk(x, meta_tokens, mix_norm, mlp_norm, w_in_even, pool_w, pool_scale, conv_qkv, a_log, dt_bias,
           delta_norm, w_out_even, w_in_odd, q_norm, k_norm, w_out_odd, w_mlp_in, w_mlp_out):
    b, s, _ = x.shape
    lp = s + LANE
    tt = _token_tile(lp)
    r = b * lp
    h = jnp.concatenate([jnp.zeros((b, FRONT, D_MODEL), F32),
                         jnp.broadcast_to(meta_tokens[None], (b, N_META, D_MODEL)), x], axis=1)
    h2 = h.reshape(r, D_MODEL)

    o1 = POOL_WIDTH
    o2 = o1 + 2 * DN_QK + DN_V
    o3 = o2 + DN_V
    w_in = w_in_even[0]
    wb = jnp.zeros((D_MODEL, LANE), F32).at[:, :4 * DN_HEADS].set(w_in[:, o3:])
    up, uq, z, ba = _in_even(h2, mix_norm[0][None], w_in[:, :o1].astype(BF16), w_in[:, o1:o2].astype(BF16),
                             w_in[:, o2:o3].astype(BF16), wb.astype(BF16), tt)
    gate_p = jnp.zeros((8, LANE), F32)
    gate_p = gate_p.at[0, 2 * DN_HEADS:4 * DN_HEADS].set(-jnp.exp(a_log[0].reshape(-1)))
    gate_p = gate_p.at[1, 2 * DN_HEADS:4 * DN_HEADS].set(dt_bias[0].reshape(-1))
    conv_w = jnp.zeros((8, 2 * DN_QK + DN_V), F32).at[:DN_CONV].set(conv_qkv[0])
    seq = lambda a: a.reshape(b, lp, a.shape[-1])
    y_pool, q, k, v, gt = _prep(seq(up), seq(uq), seq(ba), pool_w[0].astype(BF16), pool_scale[0][None], conv_w,
                                gate_p, tt)
    w, u, qd, kd, at, gl = _delta_intra(q, k, v, gt, tt)
    o_f, o_b = _delta_inter(w, u, qd, kd, at, gl, tt)
    wo = w_out_even[0].astype(BF16)
    flat = lambda a: a.reshape(r, a.shape[-1])
    h2 = _out_mlp_even(flat(y_pool), flat(o_f), flat(o_b), z, delta_norm[0][None], wo[:POOL_WIDTH], wo[POOL_WIDTH:],
                       h2, mlp_norm[0][None], w_mlp_in[0].astype(BF16), w_mlp_out[0].astype(BF16), tt)

    nq, nk = ATT_HEADS * ATT_HD, ATT_KV_HEADS * ATT_HD
    w_in = w_in_odd[0].astype(BF16)
    cos, sin = _rope_tables(lp)
    qa, ka, va = _in_odd(h2, mix_norm[1][None], w_in[:, :nq], w_in[:, nq:nq + nk], w_in[:, nq + nk:],
                         q_norm[0][None], k_norm[0][None], cos, sin, tt)
    att = _attention(seq(qa), seq(ka), seq(va), tt)
    h2 = _out_mlp_odd(flat(att), w_out_odd[0].astype(BF16), h2, mlp_norm[1][None], w_mlp_in[1].astype(BF16),
                      w_mlp_out[1].astype(BF16), tt)
    return h2.reshape(b, lp, D_MODEL)[:, LANE:]


def kernel(x_prompt, x_sample, meta_tokens, mix_norm, mlp_norm, w_in_even, pool_w, pool_scale, conv_qkv, a_log,
           dt_bias, delta_norm, w_out_even, w_in_odd, q_norm, k_norm, w_out_odd, w_mlp_in, w_mlp_out):
    nb = x_prompt.shape[0]
    x = jnp.concatenate([x_prompt, x_sample], axis=0)
    y = _trunk(x, meta_tokens, mix_norm, mlp_norm, w_in_even, pool_w, pool_scale, conv_qkv, a_log, dt_bias,
               delta_norm, w_out_even, w_in_odd, q_norm, k_norm, w_out_odd, w_mlp_in, w_mlp_out)
    return y[:nb], y[nb:]
```

```python
import functools
import math

import jax
import jax.numpy as jnp
import numpy as np
from jax import lax
from jax.experimental import pallas as pl
from jax.experimental.pallas import tpu as pltpu

F32 = jnp.float32
BF16 = jnp.bfloat16

D_MODEL = 1024
N_META = 16
LANE = 128
MXU_TILE = 256
HALO = 16
FRONT = LANE - N_META
GRID_W = 64
EPS = 1e-6
POOL_WINDOWS = (2, 4, 8, 16)
POOL_WIDTH = 512
DN_HEADS = 4
DN_CONV = 7
CHUNK = 64
PAIR = 2 * CHUNK
ROWB = 128
DN_QK = 512
DN_V = 512
ATT_HD = 128
ATT_HEADS = 8
ATT_KV_HEADS = 2
ATT_GROUP = 4
ROPE_THETA = 10000.0
D_FF = 4096
FF_CHUNK = 1024
VMEM_LIMIT = 56 * 1024 * 1024
LOG2E = math.log2(math.e)
SOFTMAX_MIN_DENOM = 1e-26


def _token_tile(lp):
    for t in (640, 512, 384, 256, 128):
        if lp % t == 0:
            return t
    raise ValueError(lp)


def _params(sem):
    return pltpu.CompilerParams(dimension_semantics=sem, vmem_limit_bytes=VMEM_LIMIT)


def _const_spec(shape):
    nd = len(shape)
    return pl.BlockSpec(shape, lambda *_: (0,) * nd, pipeline_mode=pl.Buffered(1))


def _dot(a, b):
    return jnp.dot(a, b, preferred_element_type=F32)


def _dot_nt(a, b):
    return lax.dot_general(a, b, (((1,), (1,)), ((), ())), preferred_element_type=F32)


def _dot_tn(a, b):
    return lax.dot_general(a, b, (((0,), (0,)), ((), ())), preferred_element_type=F32)


def _split2(x):
    hi = x.astype(BF16)
    lo = (x - hi.astype(F32)).astype(BF16)
    return hi, lo


def _split3(x):
    hi = x.astype(BF16)
    r = x - hi.astype(F32)
    mid = r.astype(BF16)
    lo = (r - mid.astype(F32)).astype(BF16)
    return hi, mid, lo


def _lane_sum(x, ones_bf):
    hi, lo = _split2(x)
    return _dot(hi, ones_bf) + _dot(lo, ones_bf)


def _sigmoid(x):
    return 1.0 / (1.0 + jnp.exp(-x))


def _rms_rows(x, g):
    ms = jnp.mean(x * x, axis=-1, keepdims=True)
    return x * lax.rsqrt(ms + EPS) * g


def _in_even_kernel(h_ref, g_ref, wp_ref, wq_ref, wz_ref, wb_ref, up_ref, uq_ref, z_ref, ba_ref):
    xn = _rms_rows(h_ref[...], g_ref[...]).astype(BF16)
    up_ref[...] = _dot(xn, wp_ref[...]).astype(BF16)
    uq_ref[...] = _dot(xn, wq_ref[...]).astype(BF16)
    z_ref[...] = _dot(xn, wz_ref[...]).astype(BF16)
    ba_ref[...] = _dot(xn, wb_ref[...])


def _in_even(h2, g, wp, wq, wz, wb, tt):
    r = h2.shape[0]
    row = lambda n: pl.BlockSpec((tt, n), lambda i: (i, 0))
    return pl.pallas_call(
        _in_even_kernel,
        grid=(r // tt,),
        in_specs=[row(D_MODEL), _const_spec((1, D_MODEL)), _const_spec(wp.shape), _const_spec(wq.shape),
                  _const_spec(wz.shape), _const_spec(wb.shape)],
        out_specs=[row(POOL_WIDTH), row(2 * DN_QK + DN_V), row(DN_V), row(LANE)],
        out_shape=[jax.ShapeDtypeStruct((r, POOL_WIDTH), BF16), jax.ShapeDtypeStruct((r, 2 * DN_QK + DN_V), BF16),
                   jax.ShapeDtypeStruct((r, DN_V), BF16), jax.ShapeDtypeStruct((r, LANE), F32)],
        compiler_params=_params(("parallel",)),
        name="in_proj_even",
    )(h2, g, wp, wq, wz, wb)


def _prep_kernel(up_ref, upl_ref, upr_ref, uq_ref, uql_ref, uqr_ref, ba_ref, band_ref, tri_ref, pw_ref, ps_ref,
                 cw_ref, gp_ref, yp_ref, q_ref, k_ref, v_ref, gt_ref, extp_ref, extq_ref, vb_ref, *, ts, lp):
    i = pl.program_id(1)
    row0 = i * ts
    next_ = ROWB + 2 * HALO
    rows_ext = lax.broadcasted_iota(jnp.int32, (ts + 2 * HALO, LANE), 0) + (row0 - HALO)
    valid_ext = (rows_ext >= FRONT) & (rows_ext < lp)
    ones_bf = jnp.ones((LANE, LANE), BF16)
    lane = lax.broadcasted_iota(jnp.int32, (ROWB, LANE), 1)
    n_blk = (2 * DN_QK + DN_V) // LANE

    vb_ref[...] = jnp.where(valid_ext, 1.0, 0.0).astype(BF16)
    for gi in range(len(POOL_WINDOWS)):
        sl = slice(gi * LANE, (gi + 1) * LANE)
        eg = jnp.concatenate([upl_ref[0, :, sl], up_ref[0, :, sl], upr_ref[0, :, sl]], axis=0)
        extp_ref[:, sl] = jnp.where(valid_ext, eg, jnp.zeros_like(eg))
    for cb in range(n_blk):
        sl = slice(cb * LANE, (cb + 1) * LANE)
        e = jnp.concatenate([uql_ref[0, :, sl], uq_ref[0, :, sl], uqr_ref[0, :, sl]], axis=0).astype(F32)
        extq_ref[cb] = jnp.where(valid_ext, e, 0.0)

    for rb in range(ts // ROWB):
        r0 = rb * ROWB
        rows = slice(r0, r0 + ROWB)
        valid_main = valid_ext[HALO + r0:HALO + r0 + ROWB]

        vb = vb_ref[r0:r0 + next_, :]
        scs = [_dot(band_ref[gi], jnp.concatenate([extp_ref[r0:r0 + next_, gi * LANE:(gi + 1) * LANE], vb], axis=1))
               for gi in range(len(POOL_WINDOWS))]
        ds_ = [(sc[:, :LANE] / jnp.maximum(sc[:, LANE:], 1.0)
                - up_ref[0, rows, gi * LANE:(gi + 1) * LANE].astype(F32)).astype(BF16) for gi, sc in enumerate(scs)]
        for gi, d in enumerate(ds_):
            sl = slice(gi * LANE, (gi + 1) * LANE)
            y = _dot(d, pw_ref[gi]) * ps_ref[:, sl]
            yp_ref[0, rows, sl] = jnp.where(valid_main, y, 0.0).astype(BF16)

        xs = []
        for cb in range(n_blk):
            sl = slice(cb * LANE, (cb + 1) * LANE)
            acc = jnp.zeros((ROWB, LANE), F32)
            for j in range(DN_CONV):
                acc = acc + cw_ref[j:j + 1, sl] * extq_ref[cb, pl.ds(r0 + HALO - DN_CONV // 2 + j, ROWB), :]
            xs.append(jnp.where(valid_main, acc * _sigmoid(acc), 0.0))
        sss = [_lane_sum(x * x, ones_bf) for x in xs[:2 * DN_HEADS]]
        for cb, x in enumerate(xs):
            hsl = slice((cb % DN_HEADS) * LANE, (cb % DN_HEADS + 1) * LANE)
            if cb < DN_HEADS:
                q_ref[0, rows, hsl] = (x * lax.rsqrt(sss[cb] + EPS) * (LANE ** -0.5)).astype(BF16)
            elif cb < 2 * DN_HEADS:
                k_ref[0, rows, hsl] = (x * lax.rsqrt(sss[cb] + EPS)).astype(BF16)
            else:
                v_ref[0, rows, hsl] = x.astype(BF16)

        ba = ba_ref[0, rows, :]
        beta = jnp.where(valid_main, _sigmoid(ba), 0.0)
        xg = ba + gp_ref[1:2, :]
        softplus = jnp.maximum(xg, 0.0) + jnp.log1p(jnp.exp(-jnp.abs(xg)))
        g = jnp.where(valid_main, gp_ref[0:1, :] * softplus, 0.0)
        g3 = _split3(g)
        gcf = _dot(tri_ref[0], g3[0]) + _dot(tri_ref[0], g3[1]) + _dot(tri_ref[0], g3[2])
        gcb = _dot(tri_ref[1], g3[0]) + _dot(tri_ref[1], g3[1]) + _dot(tri_ref[1], g3[2])
        gt_ref[0, rows, :] = jnp.where(lane < 2 * DN_HEADS, beta,
                                       jnp.where(lane < 3 * DN_HEADS, gcf, jnp.where(lane < 4 * DN_HEADS, gcb, 0.0)))


def _prep_constants():
    t = np.arange(ROWB)[:, None]
    j = np.arange(ROWB + 2 * HALO)[None, :] - HALO
    bands = []
    for w in POOL_WINDOWS:
        bands.append(((j >= t - w // 2) & (j <= t + (w - 1 - w // 2))).astype(np.float32))
    band = jnp.asarray(np.stack(bands), BF16)
    a = np.arange(ROWB)
    same = (a[:, None] // CHUNK) == (a[None, :] // CHUNK)
    tri = np.stack([same & (a[None, :] <= a[:, None]), same & (a[None, :] >= a[:, None])]).astype(np.float32)
    return band, jnp.asarray(tri, BF16)


def _prep(up, uq, ba, pool_w, pool_scale, conv_w, gate_p, ts):
    b, lp, _ = up.shape
    nt = lp // ts
    hb = ts // HALO
    band, tri = _prep_constants()
    cqkv = 2 * DN_QK + DN_V
    main = lambda n: pl.BlockSpec((1, ts, n), lambda bi, i: (bi, i, 0))
    left = lambda n: pl.BlockSpec((1, HALO, n), lambda bi, i: (bi, jnp.maximum(i * hb - 1, 0), 0))
    right = lambda n: pl.BlockSpec((1, HALO, n), lambda bi, i: (bi, jnp.minimum((i + 1) * hb, lp // HALO - 1), 0))
    out = lambda n, dt: jax.ShapeDtypeStruct((b, lp, n), dt)
    return pl.pallas_call(
        functools.partial(_prep_kernel, ts=ts, lp=lp),
        grid=(b, nt),
        in_specs=[main(POOL_WIDTH), left(POOL_WIDTH), right(POOL_WIDTH), main(cqkv), left(cqkv), right(cqkv),
                  main(LANE), _const_spec(band.shape), _const_spec(tri.shape), _const_spec(pool_w.shape),
                  _const_spec(pool_scale.shape), _const_spec(conv_w.shape), _const_spec(gate_p.shape)],
        out_specs=[main(POOL_WIDTH), main(DN_QK), main(DN_QK), main(DN_V), main(LANE)],
        out_shape=[out(POOL_WIDTH, BF16), out(DN_QK, BF16), out(DN_QK, BF16), out(DN_V, BF16), out(LANE, F32)],
        scratch_shapes=[pltpu.VMEM((ts + 2 * HALO, POOL_WIDTH), BF16), pltpu.VMEM((cqkv // LANE, ts + 2 * HALO, LANE), F32),
                        pltpu.VMEM((ts + 2 * HALO, LANE), BF16)],
        compiler_params=_params(("parallel", "parallel")),
        name="mixer_prep",
    )(up, up, up, uq, uq, uq, ba, band, tri, pool_w, pool_scale, conv_w, gate_p)


def _pair_masks():
    r = lax.broadcasted_iota(jnp.int32, (PAIR, PAIR), 0)
    c = lax.broadcasted_iota(jnp.int32, (PAIR, PAIR), 1)
    top = (r < CHUNK) & (c < CHUNK)
    bot = (r >= CHUNK) & (c >= CHUNK)
    incl = (top & (r >= c)) | (bot & (r <= c))
    strict = (top & (r > c)) | (bot & (r < c))
    m16 = (r // 16) == (c // 16)
    m32 = (r // 32) == (c // 32)
    eye = (r == c).astype(F32)
    return incl, strict, m16, (m32 & ~m16, ~m32), eye


def _unit_tri_inverse(lmats, m16, off_masks, eye):
    ns = [-jnp.where(m16, lm, 0.0) for lm in lmats]
    ts_ = [eye + n for n in ns]
    ps = ns
    for _ in range(3):
        pbs = [p.astype(BF16) for p in ps]
        ps = [_dot(pb, pb) for pb in pbs]
        ts_ = [t + _dot(t.astype(BF16), p.astype(BF16)) for t, p in zip(ts_, ps)]
    for lo_mask in off_masks:
        tbs = [t.astype(BF16) for t in ts_]
        offs = [jnp.where(lo_mask, lm, 0.0).astype(BF16) for lm in lmats]
        mids = [_dot(tb, off).astype(BF16) for tb, off in zip(tbs, offs)]
        ts_ = [t - _dot(mid, tb) for t, mid, tb in zip(ts_, mids, tbs)]
    return ts_


def _intra_chunks(q_ref, k_ref, v_ref, gt_ref, w_ref, u_ref, qd_ref, kd_ref, at_ref, gl_ref, chunks, masks):
    incl, strict, m16, off_masks, eye = masks
    lane_row = lax.broadcasted_iota(jnp.int32, (1, PAIR), 1)
    sub_col = lax.broadcasted_iota(jnp.int32, (PAIR, 1), 0)
    nh = DN_HEADS
    inst = []
    for j in chunks:
        rows = pl.ds(pl.multiple_of(j * CHUNK, CHUNK), CHUNK)
        gt = gt_ref[0, rows, :]
        g2t = jnp.concatenate([gt, gt], axis=0).T
        for h in range(nh):
            sl = slice(h * LANE, (h + 1) * LANE)
            qc, kc, vc = q_ref[0, rows, sl], k_ref[0, rows, sl], v_ref[0, rows, sl]
            k2, q2 = jnp.concatenate([kc, kc], axis=0), jnp.concatenate([qc, qc], axis=0)
            k2f = k2.astype(F32)
            bcol = jnp.concatenate([gt[:, h:h + 1], gt[:, nh + h:nh + h + 1]], axis=0)
            gcol = jnp.concatenate([gt[:, 2 * nh + h:2 * nh + h + 1], gt[:, 3 * nh + h:3 * nh + h + 1]], axis=0)
            grow = jnp.where(lane_row < CHUNK, g2t[2 * nh + h:2 * nh + h + 1, :],
                             g2t[3 * nh + h:3 * nh + h + 1, :])
            kb2 = k2f * bcol
            inst.append(dict(j=j, h=h, rows=rows, k2=k2, q2=q2, k2f=k2f, vc=vc, bcol=bcol, gcol=gcol, grow=grow,
                             kb2=kb2))
    raws = [_dot_nt(jnp.concatenate([it["kb2"].astype(BF16), it["q2"]], axis=0), it["k2"]) for it in inst]
    decays = [jnp.exp(jnp.where(incl, it["gcol"] - it["grow"], -jnp.inf)) for it in inst]
    lmats = [jnp.where(strict, raw[:PAIR] * dec, 0.0) for raw, dec in zip(raws, decays)]
    tinv = _unit_tri_inverse(lmats, m16, off_masks, eye)
    rhss = []
    for it in inst:
        e2 = jnp.exp(it["gcol"])
        v2f = jnp.concatenate([it["vc"], it["vc"]], axis=0).astype(F32)
        rhss.append(jnp.concatenate([v2f * it["bcol"], it["kb2"] * e2], axis=1).astype(BF16))
        it["e2"] = e2
    uws = [_dot(t.astype(BF16), rhs) for t, rhs in zip(tinv, rhss)]
    for it, raw, dec, uw in zip(inst, raws, decays, uws):
        h, rows, gcol = it["h"], it["rows"], it["gcol"]
        attn = raw[PAIR:] * dec
        qd2 = it["q2"].astype(F32) * it["e2"]
        gl_col = jnp.where(sub_col < CHUNK, gcol[CHUNK - 1:CHUNK, :], gcol[CHUNK:CHUNK + 1, :])
        kd2 = it["k2f"] * jnp.exp(gl_col - gcol)
        for d in range(2):
            half = slice(d * CHUNK, (d + 1) * CHUNK)
            dst = slice((d * nh + h) * LANE, (d * nh + h + 1) * LANE)
            u_ref[0, rows, dst] = uw[half, :LANE].astype(BF16)
            w_ref[0, rows, dst] = uw[half, LANE:].astype(BF16)
            qd_ref[0, rows, dst] = qd2[half].astype(BF16)
            kd_ref[0, rows, dst] = kd2[half].astype(BF16)
            at_ref[0, rows, dst] = attn[half].astype(BF16)
            g_last = gcol[CHUNK - 1:CHUNK, :] if d == 0 else gcol[CHUNK:CHUNK + 1, :]
            gl_ref[0, it["j"], d * nh + h:d * nh + h + 1, :] = jnp.broadcast_to(jnp.exp(g_last), (1, LANE))


def _delta_intra_kernel(q_ref, k_ref, v_ref, gt_ref, w_ref, u_ref, qd_ref, kd_ref, at_ref, gl_ref, *, ts):
    masks = _pair_masks()

    ncb = ts // CHUNK
    per_body = next(n for n in (5, 4, 3, 2, 1) if ncb % n == 0)

    def body(jj, _):
        _intra_chunks(q_ref, k_ref, v_ref, gt_ref, w_ref, u_ref, qd_ref, kd_ref, at_ref, gl_ref,
                      tuple(per_body * jj + i for i in range(per_body)), masks)
        return 0

    lax.fori_loop(0, ncb // per_body, body, 0)


def _delta_intra(q, k, v, gt, ts):
    b, lp, _ = q.shape
    ncb = ts // CHUNK
    wide = 2 * DN_HEADS * LANE
    tile = lambda n: pl.BlockSpec((1, ts, n), lambda bi, i: (bi, i, 0))
    big = jax.ShapeDtypeStruct((b, lp, wide), BF16)
    return pl.pallas_call(
        functools.partial(_delta_intra_kernel, ts=ts),
        grid=(b, lp // ts),
        in_specs=[tile(DN_QK), tile(DN_QK), tile(DN_V), tile(LANE)],
        out_specs=[tile(wide)] * 5 + [pl.BlockSpec((1, ncb, 2 * DN_HEADS, LANE), lambda bi, i: (bi, i, 0, 0))],
        out_shape=[big] * 5 + [jax.ShapeDtypeStruct((b, lp // CHUNK, 2 * DN_HEADS, LANE), F32)],
        compiler_params=_params(("parallel", "parallel")),
        name="delta_intra",
    )(q, k, v, gt)


def _delta_inter_kernel(wf_ref, wb_ref, uf_ref, ub_ref, qdf_ref, qdb_ref, kdf_ref, kdb_ref, atf_ref, atb_ref,
                        glf_ref, glb_ref, of_ref, ob_ref, s_ref, *, ts, nb):
    ncb = ts // CHUNK
    nh = DN_HEADS

    @pl.when(pl.program_id(1) == 0)
    def _():
        s_ref[...] = jnp.zeros(s_ref.shape, F32)

    zero = jnp.zeros((CHUNK, LANE), BF16)
    left = lambda x: jnp.concatenate([x, zero], axis=1)
    right = lambda x: jnp.concatenate([zero, x], axis=1)

    chains = [(bb, h, slice(h * LANE, (h + 1) * LANE)) for bb in range(nb) for h in range(nh)]

    def body(j, _):
        jb = ncb - 1 - j
        rf = pl.ds(pl.multiple_of(j * CHUNK, CHUNK), CHUNK)
        rb = pl.ds(pl.multiple_of(jb * CHUNK, CHUNK), CHUNK)
        s2s = [s_ref[bb * nh + h] for bb, h, _ in chains]
        lhss = [jnp.concatenate([left(wf_ref[bb, rf, sl]), right(wb_ref[bb, rb, sl]),
                                 left(qdf_ref[bb, rf, sl]), right(qdb_ref[bb, rb, sl])], axis=0)
                for bb, _, sl in chains]
        r1s = [_dot(lhs, s2.astype(BF16)) for lhs, s2 in zip(lhss, s2s)]
        v_news = [(jnp.concatenate([uf_ref[bb, rf, sl], ub_ref[bb, rb, sl]], axis=0).astype(F32) - r1[:PAIR])
                  .astype(BF16) for (bb, _, sl), r1 in zip(chains, r1s)]
        o2s = [r1[PAIR:] + _dot(jnp.concatenate([atf_ref[bb, rf, sl], atb_ref[bb, rb, sl]], axis=0), v_new)
               for (bb, _, sl), r1, v_new in zip(chains, r1s, v_news)]
        upds = [_dot_tn(jnp.concatenate([left(kdf_ref[bb, rf, sl]), right(kdb_ref[bb, rb, sl])], axis=0), v_new)
                for (bb, _, sl), v_new in zip(chains, v_news)]
        for (bb, h, sl), s2, o2, upd in zip(chains, s2s, o2s, upds):
            of_ref[bb, rf, sl] = o2[:CHUNK].astype(BF16)
            ob_ref[bb, rb, sl] = o2[CHUNK:].astype(BF16)
            scale = jnp.concatenate([jnp.broadcast_to(glf_ref[bb, j, h:h + 1, :], (LANE, LANE)),
                                     jnp.broadcast_to(glb_ref[bb, jb, nh + h:nh + h + 1, :], (LANE, LANE))], axis=0)
            s_ref[bb * nh + h] = s2 * scale + upd
        return 0

    lax.fori_loop(0, ncb, body, 0)


def _delta_inter(w, u, qd, kd, at, gl, ts):
    b, lp, _ = w.shape
    nt = lp // ts
    ncb = ts // CHUNK
    half = DN_HEADS * LANE
    nb = 2 if b % 2 == 0 else 1
    fwd = pl.BlockSpec((nb, ts, half), lambda bi, i: (bi, i, 0))
    bwd = pl.BlockSpec((nb, ts, half), lambda bi, i: (bi, nt - 1 - i, 1))
    glf = pl.BlockSpec((nb, ncb, 2 * DN_HEADS, LANE), lambda bi, i: (bi, i, 0, 0))
    glb = pl.BlockSpec((nb, ncb, 2 * DN_HEADS, LANE), lambda bi, i: (bi, nt - 1 - i, 0, 0))
    out = jax.ShapeDtypeStruct((b, lp, half), BF16)
    return pl.pallas_call(
        functools.partial(_delta_inter_kernel, ts=ts, nb=nb),
        grid=(b // nb, nt),
        in_specs=[fwd, bwd] * 5 + [glf, glb],
        out_specs=[pl.BlockSpec((nb, ts, half), lambda bi, i: (bi, i, 0)),
                   pl.BlockSpec((nb, ts, half), lambda bi, i: (bi, nt - 1 - i, 0))],
        out_shape=[out, out],
        scratch_shapes=[pltpu.VMEM((nb * DN_HEADS, 2 * LANE, LANE), F32)],
        compiler_params=_params(("parallel", "arbitrary")),
        name="delta_inter",
    )(w, w, u, u, qd, qd, kd, kd, at, at, gl, gl)


def _mlp_tail(h1, g_ref, w1_ref, w2_ref, o_ref):
    hn = _rms_rows(h1, g_ref[...]).astype(BF16)
    acc = jnp.zeros_like(h1)
    for c in range(D_FF // FF_CHUNK):
        sl = slice(c * FF_CHUNK, (c + 1) * FF_CHUNK)
        hid = jnp.maximum(_dot(hn, w1_ref[:, sl]), 0.0)
        acc = acc + _dot((hid * hid).astype(BF16), w2_ref[sl, :])
    o_ref[...] = h1 + acc


def _out_mlp_even_kernel(yp_ref, of_ref, ob_ref, z_ref, ng_ref, wop_ref, wod_ref, h_ref, g_ref, w1_ref, w2_ref,
                         o_ref):
    ones_bf = jnp.ones((LANE, LANE), BF16)
    h1 = h_ref[...] + _dot(yp_ref[...], wop_ref[...])
    for hh in range(DN_HEADS):
        sl = slice(hh * LANE, (hh + 1) * LANE)
        o = of_ref[:, sl].astype(F32) + ob_ref[:, sl].astype(F32)
        ms = _lane_sum(o * o, ones_bf) * (1.0 / LANE)
        zf = z_ref[:, sl].astype(F32)
        y = o * lax.rsqrt(ms + EPS) * ng_ref[...] * (zf * _sigmoid(zf))
        h1 = h1 + _dot(y.astype(BF16), wod_ref[sl, :])
    _mlp_tail(h1, g_ref, w1_ref, w2_ref, o_ref)


def _out_mlp_even(yp, of, ob, z, ng, wop, wod, h2, g, w1, w2, tt):
    r = h2.shape[0]
    row = lambda n: pl.BlockSpec((tt, n), lambda i: (i, 0))
    return pl.pallas_call(
        _out_mlp_even_kernel,
        grid=(r // tt,),
        in_specs=[row(POOL_WIDTH), row(DN_V), row(DN_V), row(DN_V), _const_spec((1, LANE)), _const_spec(wop.shape),
                  _const_spec(wod.shape), row(D_MODEL), _const_spec((1, D_MODEL)), _const_spec(w1.shape),
                  _const_spec(w2.shape)],
        out_specs=row(D_MODEL),
        out_shape=jax.ShapeDtypeStruct((r, D_MODEL), F32),
        compiler_params=_params(("parallel",)),
        name="out_proj_mlp_even",
    )(yp, of, ob, z, ng, wop, wod, h2, g, w1, w2)


def _out_mlp_odd_kernel(a_ref, wo_ref, h_ref, g_ref, w1_ref, w2_ref, o_ref):
    h1 = h_ref[...] + _dot(a_ref[...], wo_ref[...])
    _mlp_tail(h1, g_ref, w1_ref, w2_ref, o_ref)


def _out_mlp_odd(a, wo, h2, g, w1, w2, tt):
    r = h2.shape[0]
    row = lambda n: pl.BlockSpec((tt, n), lambda i: (i, 0))
    return pl.pallas_call(
        _out_mlp_odd_kernel,
        grid=(r // tt,),
        in_specs=[row(a.shape[1]), _const_spec(wo.shape), row(D_MODEL), _const_spec((1, D_MODEL)),
                  _const_spec(w1.shape), _const_spec(w2.shape)],
        out_specs=row(D_MODEL),
        out_shape=jax.ShapeDtypeStruct((r, D_MODEL), F32),
        compiler_params=_params(("parallel",)),
        name="out_proj_mlp_odd",
    )(a, wo, h2, g, w1, w2)


def _in_odd_kernel(h_ref, g_ref, wq_ref, wk_ref, wv_ref, qn_ref, kn_ref, cos_ref, sin_ref, q_ref, k_ref, v_ref):
    xn = _rms_rows(h_ref[...], g_ref[...]).astype(BF16)
    ones_bf = jnp.ones((LANE, LANE), BF16)
    cos = cos_ref[...]
    sin = sin_ref[...]

    def norm_rope(x, gain, scale):
        ms = _lane_sum(x * x, ones_bf) * (1.0 / ATT_HD)
        y = x * lax.rsqrt(ms + EPS) * gain
        return (y * cos + pltpu.roll(y, ATT_HD // 2, axis=1) * sin) * scale

    uq = _dot(xn, wq_ref[...])
    for hh in range(ATT_HEADS):
        sl = slice(hh * ATT_HD, (hh + 1) * ATT_HD)
        q_ref[:, sl] = norm_rope(uq[:, sl], qn_ref[...], ATT_HD ** -0.5 * LOG2E).astype(BF16)
    uk = _dot(xn, wk_ref[...])
    for hh in range(ATT_KV_HEADS):
        sl = slice(hh * ATT_HD, (hh + 1) * ATT_HD)
        k_ref[:, sl] = norm_rope(uk[:, sl], kn_ref[...], 1.0).astype(BF16)
    v_ref[...] = _dot(xn, wv_ref[...]).astype(BF16)


def _in_odd(h2, g, wq, wk, wv, qn, kn, cos, sin, tt):
    r = h2.shape[0]
    ntb = cos.shape[0] // tt
    row = lambda n: pl.BlockSpec((tt, n), lambda i: (i, 0))
    tab = pl.BlockSpec((tt, ATT_HD), lambda i: (i % ntb, 0))
    nq, nk = ATT_HEADS * ATT_HD, ATT_KV_HEADS * ATT_HD
    return pl.pallas_call(
        _in_odd_kernel,
        grid=(r // tt,),
        in_specs=[row(D_MODEL), _const_spec((1, D_MODEL)), _const_spec(wq.shape), _const_spec(wk.shape),
                  _const_spec(wv.shape), _const_spec((1, ATT_HD)), _const_spec((1, ATT_HD)), tab, tab],
        out_specs=[row(nq), row(nk), row(nk)],
        out_shape=[jax.ShapeDtypeStruct((r, nq), BF16), jax.ShapeDtypeStruct((r, nk), BF16),
                   jax.ShapeDtypeStruct((r, nk), BF16)],
        compiler_params=_params(("parallel",)),
        name="in_proj_odd",
    )(h2, g, wq, wk, wv, qn, kn, cos, sin)


def _attn_kernel(q_ref, k_ref, v_ref, o_ref, kp_ref, vaug_ref, kmax_ref, qs_ref, b_ref, acc_ref, *, tqb, tk, lp):
    sub = LANE
    n_sub = tqb // sub
    lpk = kp_ref.shape[0]
    n_kc = lpk // tk
    rows_q = ATT_GROUP * sub
    ones_bf = jnp.ones((LANE, LANE), BF16)

    @pl.when(pl.program_id(2) == 0)
    def _():
        def fill(c, kmax):
            rows = pl.ds(pl.multiple_of(c * tqb, tqb), tqb)
            valid = (lax.broadcasted_iota(jnp.int32, (tqb, LANE), 0) + c * tqb) >= FRONT
            vaug_ref[rows, :ATT_HD] = jnp.where(valid, v_ref[0, rows, :], jnp.zeros((tqb, ATT_HD), BF16))
            vaug_ref[rows, ATT_HD:] = jnp.where(valid, 1.0, 0.0).astype(BF16)
            kc = k_ref[0, rows, :]
            kp_ref[rows, :] = kc
            kf = kc.astype(F32)
            n2 = jnp.where(valid, _lane_sum(kf * kf, ones_bf), 0.0)
            return jnp.maximum(kmax, jnp.max(n2, axis=0, keepdims=True))

        kmax = lax.fori_loop(0, lp // tqb, fill, jnp.zeros((1, LANE), F32))
        kmax_ref[...] = jnp.broadcast_to(kmax, kmax_ref.shape)
        if lpk > lp:
            kp_ref[lp:, :] = jnp.zeros((lpk - lp, ATT_HD), BF16)
            vaug_ref[lp:, :] = jnp.zeros((lpk - lp, 2 * ATT_HD), BF16)

    kmax2 = kmax_ref[0:1, :]
    for si in range(n_sub):
        q4 = q_ref[0, si * sub:(si + 1) * sub, :]
        qs = jnp.concatenate([q4[:, g * ATT_HD:(g + 1) * ATT_HD] for g in range(ATT_GROUP)], axis=0)
        qs_ref[si] = qs
        qf = qs.astype(F32)
        b_ref[si] = jnp.sqrt(_lane_sum(qf * qf, ones_bf) * kmax2) * (1.0 + 2.0 ** -10)
        acc_ref[si] = jnp.zeros((rows_q, 2 * ATT_HD), F32)

    def fast(c, _):
        rows = pl.ds(pl.multiple_of(c * tk, tk), tk)
        kc = kp_ref[rows, :]
        va = vaug_ref[rows, :]
        for si in range(n_sub):
            s = _dot_nt(qs_ref[si], kc)
            p = jnp.exp2(s - pltpu.repeat(b_ref[si], tk // LANE, axis=1)).astype(BF16)
            acc_ref[si] += _dot(p, va)
        return 0

    lax.fori_loop(0, n_kc, fast, 0, unroll=True)

    def write(si, acc):
        out = (acc[:, :ATT_HD] / acc[:, ATT_HD:]).astype(BF16)
        start = si * sub if isinstance(si, int) else pl.multiple_of(si * sub, sub)
        for g in range(ATT_GROUP):
            o_ref[0, pl.ds(start, sub), g * ATT_HD:(g + 1) * ATT_HD] = out[g * sub:(g + 1) * sub]

    l_min = None
    for si in range(n_sub):
        acc = acc_ref[si]
        write(si, acc)
        m = jnp.min(acc[:, ATT_HD:])
        l_min = m if l_min is None else jnp.minimum(l_min, m)

    @pl.when(jnp.logical_not(l_min >= SOFTMAX_MIN_DENOM))
    def _():
        col = lax.broadcasted_iota(jnp.int32, (rows_q, tk), 1)

        def redo(si, _):
            qs = qs_ref[si]

            def step(c, carry):
                m, acc = carry
                rows = pl.ds(pl.multiple_of(c * tk, tk), tk)
                key = col + c * tk
                s = jnp.where((key >= FRONT) & (key < lp), _dot_nt(qs, kp_ref[rows, :]), -jnp.inf)
                m_new = jnp.maximum(m, jnp.max(s, axis=1, keepdims=True))
                p = jnp.exp2(s - m_new).astype(BF16)
                return m_new, jnp.exp2(m - m_new) * acc + _dot(p, vaug_ref[rows, :])

            init = (jnp.full((rows_q, 1), -jnp.inf, F32), jnp.zeros((rows_q, 2 * ATT_HD), F32))
            _, acc = lax.fori_loop(0, n_kc, step, init)
            write(si, acc)
            return 0

        lax.fori_loop(0, n_sub, redo, 0)


def _attention(q, k, v, tqb):
    b, lp, _ = q.shape
    gw = ATT_GROUP * ATT_HD
    n_sub = tqb // LANE
    rows_q = ATT_GROUP * LANE
    qspec = pl.BlockSpec((1, tqb, gw), lambda bi, hi, i: (bi, i, hi))
    kvspec = pl.BlockSpec((1, lp, ATT_HD), lambda bi, hi, i: (bi, 0, hi))
    tk = min(3 * MXU_TILE, -(-lp // MXU_TILE) * MXU_TILE)
    lpk = -(-lp // tk) * tk
    return pl.pallas_call(
        functools.partial(_attn_kernel, tqb=tqb, tk=tk, lp=lp),
        grid=(b, ATT_KV_HEADS, lp // tqb),
        in_specs=[qspec, kvspec, kvspec],
        out_specs=qspec,
        out_shape=jax.ShapeDtypeStruct((b, lp, ATT_HEADS * ATT_HD), BF16),
        scratch_shapes=[pltpu.VMEM((lpk, ATT_HD), BF16), pltpu.VMEM((lpk, 2 * ATT_HD), BF16),
                        pltpu.VMEM((8, LANE), F32),
                        pltpu.VMEM((n_sub, rows_q, ATT_HD), BF16), pltpu.VMEM((n_sub, rows_q, LANE), F32),
                        pltpu.VMEM((n_sub, rows_q, 2 * ATT_HD), F32)],
        compiler_params=_params(("parallel", "parallel", "arbitrary")),
        name="gqa_attention",
    )(q, k, v)


def _rope_tables(lp):
    s = lp - LANE
    pos = np.arange(s)
    freqs = jnp.asarray(ROPE_THETA, F32) ** (-(jnp.arange(ATT_HD // 4, dtype=F32) / (ATT_HD // 4)))
    row = jnp.asarray(pos // GRID_W, F32)
    colp = jnp.asarray(pos % GRID_W, F32)
    ang = jnp.concatenate([row[:, None] * freqs, colp[:, None] * freqs], axis=-1)
    ang = jnp.concatenate([jnp.zeros((LANE, ATT_HD // 2), F32), ang], axis=0)
    c, sn = jnp.cos(ang), jnp.sin(ang)
    return jnp.concatenate([c, c], axis=-1), jnp.concatenate([-sn, sn], axis=-1)


def _trunk(x, meta_tokens, mix_norm, mlp_norm, w_in_even, pool_w, pool_scale, conv_qkv, a_log, dt_bias,
           delta_norm, w_out_even, w_in_odd, q_norm, k_norm, w_out_odd, w_mlp_in, w_mlp_out):
    b, s, _ = x.shape
    lp = s + LANE
    tt = _token_tile(lp)
    r = b * lp
    h = jnp.concatenate([jnp.zeros((b, FRONT, D_MODEL), F32),
                         jnp.broadcast_to(meta_tokens[None], (b, N_META, D_MODEL)), x], axis=1)
    h2 = h.reshape(r, D_MODEL)

    o1 = POOL_WIDTH
    o2 = o1 + 2 * DN_QK + DN_V
    o3 = o2 + DN_V
    w_in = w_in_even[0]
    wb = jnp.zeros((D_MODEL, LANE), F32).at[:, :4 * DN_HEADS].set(w_in[:, o3:])
    up, uq, z, ba = _in_even(h2, mix_norm[0][None], w_in[:, :o1].astype(BF16), w_in[:, o1:o2].astype(BF16),
                             w_in[:, o2:o3].astype(BF16), wb.astype(BF16), tt)
    gate_p = jnp.zeros((8, LANE), F32)
    gate_p = gate_p.at[0, 2 * DN_HEADS:4 * DN_HEADS].set(-jnp.exp(a_log[0].reshape(-1)))
    gate_p = gate_p.at[1, 2 * DN_HEADS:4 * DN_HEADS].set(dt_bias[0].reshape(-1))
    conv_w = jnp.zeros((8, 2 * DN_QK + DN_V), F32).at[:DN_CONV].set(conv_qkv[0])
    seq = lambda a: a.reshape(b, lp, a.shape[-1])
    y_pool, q, k, v, gt = _prep(seq(up), seq(uq), seq(ba), pool_w[0].astype(BF16), pool_scale[0][None], conv_w,
                                gate_p, tt)
    w, u, qd, kd, at, gl = _delta_intra(q, k, v, gt, tt)
    o_f, o_b = _delta_inter(w, u, qd, kd, at, gl, tt)
    wo = w_out_even[0].astype(BF16)
    flat = lambda a: a.reshape(r, a.shape[-1])
    h2 = _out_mlp_even(flat(y_pool), flat(o_f), flat(o_b), z, delta_norm[0][None], wo[:POOL_WIDTH], wo[POOL_WIDTH:],
                       h2, mlp_norm[0][None], w_mlp_in[0].astype(BF16), w_mlp_out[0].astype(BF16), tt)

    nq, nk = ATT_HEADS * ATT_HD, ATT_KV_HEADS * ATT_HD
    w_in = w_in_odd[0].astype(BF16)
    cos, sin = _rope_tables(lp)
    qa, ka, va = _in_odd(h2, mix_norm[1][None], w_in[:, :nq], w_in[:, nq:nq + nk], w_in[:, nq + nk:],
                         q_norm[0][None], k_norm[0][None], cos, sin, tt)
    att = _attention(seq(qa), seq(ka), seq(va), tt)
    h2 = _out_mlp_odd(flat(att), w_out_odd[0].astype(BF16), h2, mlp_norm[1][None], w_mlp_in[1].astype(BF16),
                      w_mlp_out[1].astype(BF16), tt)
    return h2.reshape(b, lp, D_MODEL)[:, LANE:]


def kernel(x_prompt, x_sample, meta_tokens, mix_norm, mlp_norm, w_in_even, pool_w, pool_scale, conv_qkv, a_log,
           dt_bias, delta_norm, w_out_even, w_in_odd, q_norm, k_norm, w_out_odd, w_mlp_in, w_mlp_out):
    nb = x_prompt.shape[0]
    x = jnp.concatenate([x_prompt, x_sample], axis=0)
    y = _trunk(x, meta_tokens, mix_norm, mlp_norm, w_in_even, pool_w, pool_scale, conv_qkv, a_log, dt_bias,
               delta_norm, w_out_even, w_in_odd, q_norm, k_norm, w_out_odd, w_mlp_in, w_mlp_out)
    return y[:nb], y[nb:]
```

```python
import functools
import math

import jax
import jax.numpy as jnp
import numpy as np
from jax import lax
from jax.experimental import pallas as pl
from jax.experimental.pallas import tpu as pltpu

F32 = jnp.float32
BF16 = jnp.bfloat16

D_MODEL = 1024
N_META = 16
LANE = 128
MXU_TILE = 256
HALO = 16
FRONT = LANE - N_META
GRID_W = 64
EPS = 1e-6
POOL_WINDOWS = (2, 4, 8, 16)
POOL_WIDTH = 512
DN_HEADS = 4
DN_CONV = 7
CHUNK = 64
PAIR = 2 * CHUNK
ROWB = 128
DN_QK = 512
DN_V = 512
ATT_HD = 128
ATT_HEADS = 8
ATT_KV_HEADS = 2
ATT_GROUP = 4
ROPE_THETA = 10000.0
D_FF = 4096
FF_CHUNK = 1024
VMEM_LIMIT = 56 * 1024 * 1024
LOG2E = math.log2(math.e)
SOFTMAX_MIN_DENOM = 1e-26


def _token_tile(lp):
    for t in (640, 512, 384, 256, 128):
        if lp % t == 0:
            return t
    raise ValueError(lp)


def _params(sem):
    return pltpu.CompilerParams(dimension_semantics=sem, vmem_limit_bytes=VMEM_LIMIT)


def _const_spec(shape):
    nd = len(shape)
    return pl.BlockSpec(shape, lambda *_: (0,) * nd, pipeline_mode=pl.Buffered(1))


def _dot(a, b):
    return jnp.dot(a, b, preferred_element_type=F32)


def _dot_nt(a, b):
    return lax.dot_general(a, b, (((1,), (1,)), ((), ())), preferred_element_type=F32)


def _dot_tn(a, b):
    return lax.dot_general(a, b, (((0,), (0,)), ((), ())), preferred_element_type=F32)


def _split2(x):
    hi = x.astype(BF16)
    lo = (x - hi.astype(F32)).astype(BF16)
    return hi, lo


def _split3(x):
    hi = x.astype(BF16)
    r = x - hi.astype(F32)
    mid = r.astype(BF16)
    lo = (r - mid.astype(F32)).astype(BF16)
    return hi, mid, lo


def _lane_sum(x, ones_bf):
    hi, lo = _split2(x)
    return _dot(hi, ones_bf) + _dot(lo, ones_bf)


def _sigmoid(x):
    return 1.0 / (1.0 + jnp.exp(-x))


def _rms_rows(x, g):
    ms = jnp.mean(x * x, axis=-1, keepdims=True)
    return x * lax.rsqrt(ms + EPS) * g


def _in_even_kernel(*refs, n_blk, nt, nb0):
    xa_refs, xb_refs = refs[:n_blk], refs[n_blk:2 * n_blk]
    mp_ref, g_ref, wp_ref, wq_ref, wz_ref, wb_ref, h_ref, up_ref, uq_ref, z_ref, ba_ref = refs[2 * n_blk:]
    i = pl.program_id(0)
    first_group = (i // nt) < nb0
    pieces = [jnp.where(first_group, xa[0], xb[0]) for xa, xb in zip(xa_refs, xb_refs)]
    pieces[0] = jnp.where((i % nt) == 0, mp_ref[...], pieces[0])
    h = jnp.concatenate(pieces, axis=0)
    h_ref[...] = h
    xn = _rms_rows(h, g_ref[...]).astype(BF16)
    up_ref[...] = _dot(xn, wp_ref[...]).astype(BF16)
    uq_ref[...] = _dot(xn, wq_ref[...]).astype(BF16)
    z_ref[...] = _dot(xn, wz_ref[...]).astype(BF16)
    ba_ref[...] = _dot(xn, wb_ref[...])


def _in_even(xa, xb, meta_pad, g, wp, wq, wz, wb, tt):
    nb0, s, _ = xa.shape
    nb1 = xb.shape[0]
    lp = s + LANE
    nt = lp // tt
    n_blk = tt // LANE
    r = (nb0 + nb1) * lp

    def xspec(k, g0, nbg):
        def index(i):
            bi, j = i // nt, i % nt
            mine = (bi >= g0) & (bi < g0 + nbg)
            return (jnp.clip(bi - g0, 0, nbg - 1), jnp.where(mine, jnp.maximum(n_blk * j + k - 1, 0), 0), 0)
        return pl.BlockSpec((1, LANE, D_MODEL), index)

    row = lambda n: pl.BlockSpec((tt, n), lambda i: (i, 0))
    return pl.pallas_call(
        functools.partial(_in_even_kernel, n_blk=n_blk, nt=nt, nb0=nb0),
        grid=(r // tt,),
        in_specs=[xspec(k, 0, nb0) for k in range(n_blk)] + [xspec(k, nb0, nb1) for k in range(n_blk)]
        + [_const_spec((LANE, D_MODEL)), _const_spec((1, D_MODEL)), _const_spec(wp.shape), _const_spec(wq.shape),
           _const_spec(wz.shape), _const_spec(wb.shape)],
        out_specs=[row(D_MODEL), row(POOL_WIDTH), row(2 * DN_QK + DN_V), row(DN_V), row(LANE)],
        out_shape=[jax.ShapeDtypeStruct((r, D_MODEL), F32), jax.ShapeDtypeStruct((r, POOL_WIDTH), BF16),
                   jax.ShapeDtypeStruct((r, 2 * DN_QK + DN_V), BF16), jax.ShapeDtypeStruct((r, DN_V), BF16),
                   jax.ShapeDtypeStruct((r, LANE), F32)],
        compiler_params=_params(("parallel",)),
        name="in_proj_even",
    )(*([xa] * n_blk), *([xb] * n_blk), meta_pad, g, wp, wq, wz, wb)


def _prep_kernel(up_ref, upl_ref, upr_ref, uq_ref, uql_ref, uqr_ref, ba_ref, band_ref, tri_ref, pw_ref, ps_ref,
                 cw_ref, gp_ref, yp_ref, q_ref, k_ref, v_ref, gt_ref, extp_ref, extq_ref, vb_ref, *, ts, lp):
    i = pl.program_id(1)
    row0 = i * ts
    next_ = ROWB + 2 * HALO
    rows_ext = lax.broadcasted_iota(jnp.int32, (ts + 2 * HALO, LANE), 0) + (row0 - HALO)
    valid_ext = (rows_ext >= FRONT) & (rows_ext < lp)
    ones_bf = jnp.ones((LANE, LANE), BF16)
    lane = lax.broadcasted_iota(jnp.int32, (ROWB, LANE), 1)
    n_blk = (2 * DN_QK + DN_V) // LANE

    vb_ref[...] = jnp.where(valid_ext, 1.0, 0.0).astype(BF16)
    for gi in range(len(POOL_WINDOWS)):
        sl = slice(gi * LANE, (gi + 1) * LANE)
        eg = jnp.concatenate([upl_ref[0, :, sl], up_ref[0, :, sl], upr_ref[0, :, sl]], axis=0)
        extp_ref[:, sl] = jnp.where(valid_ext, eg, jnp.zeros_like(eg))
    for cb in range(n_blk):
        sl = slice(cb * LANE, (cb + 1) * LANE)
        e = jnp.concatenate([uql_ref[0, :, sl], uq_ref[0, :, sl], uqr_ref[0, :, sl]], axis=0).astype(F32)
        extq_ref[cb] = jnp.where(valid_ext, e, 0.0)

    for rb in range(ts // ROWB):
        r0 = rb * ROWB
        rows = slice(r0, r0 + ROWB)
        valid_main = valid_ext[HALO + r0:HALO + r0 + ROWB]

        vb = vb_ref[r0:r0 + next_, :]
        scs = [_dot(band_ref[gi], jnp.concatenate([extp_ref[r0:r0 + next_, gi * LANE:(gi + 1) * LANE], vb], axis=1))
               for gi in range(len(POOL_WINDOWS))]
        ds_ = [(sc[:, :LANE] / jnp.maximum(sc[:, LANE:], 1.0)
                - up_ref[0, rows, gi * LANE:(gi + 1) * LANE].astype(F32)).astype(BF16) for gi, sc in enumerate(scs)]
        for gi, d in enumerate(ds_):
            sl = slice(gi * LANE, (gi + 1) * LANE)
            y = _dot(d, pw_ref[gi]) * ps_ref[:, sl]
            yp_ref[0, rows, sl] = jnp.where(valid_main, y, 0.0).astype(BF16)

        xs = []
        for cb in range(n_blk):
            sl = slice(cb * LANE, (cb + 1) * LANE)
            acc = jnp.zeros((ROWB, LANE), F32)
            for j in range(DN_CONV):
                acc = acc + cw_ref[j:j + 1, sl] * extq_ref[cb, pl.ds(r0 + HALO - DN_CONV // 2 + j, ROWB), :]
            xs.append(jnp.where(valid_main, acc * _sigmoid(acc), 0.0))
        sss = [_lane_sum(x * x, ones_bf) for x in xs[:2 * DN_HEADS]]
        for cb, x in enumerate(xs):
            hsl = slice((cb % DN_HEADS) * LANE, (cb % DN_HEADS + 1) * LANE)
            if cb < DN_HEADS:
                q_ref[0, rows, hsl] = (x * lax.rsqrt(sss[cb] + EPS) * (LANE ** -0.5)).astype(BF16)
            elif cb < 2 * DN_HEADS:
                k_ref[0, rows, hsl] = (x * lax.rsqrt(sss[cb] + EPS)).astype(BF16)
            else:
                v_ref[0, rows, hsl] = x.astype(BF16)

        ba = ba_ref[0, rows, :]
        beta = jnp.where(valid_main, _sigmoid(ba), 0.0)
        xg = ba + gp_ref[1:2, :]
        softplus = jnp.maximum(xg, 0.0) + jnp.log1p(jnp.exp(-jnp.abs(xg)))
        g = jnp.where(valid_main, gp_ref[0:1, :] * softplus, 0.0)
        g3 = _split3(g)
        gcf = _dot(tri_ref[0], g3[0]) + _dot(tri_ref[0], g3[1]) + _dot(tri_ref[0], g3[2])
        gcb = _dot(tri_ref[1], g3[0]) + _dot(tri_ref[1], g3[1]) + _dot(tri_ref[1], g3[2])
        gt_ref[0, rows, :] = jnp.where(lane < 2 * DN_HEADS, beta,
                                       jnp.where(lane < 3 * DN_HEADS, gcf, jnp.where(lane < 4 * DN_HEADS, gcb, 0.0)))


def _prep_constants():
    t = np.arange(ROWB)[:, None]
    j = np.arange(ROWB + 2 * HALO)[None, :] - HALO
    bands = []
    for w in POOL_WINDOWS:
        bands.append(((j >= t - w // 2) & (j <= t + (w - 1 - w // 2))).astype(np.float32))
    band = jnp.asarray(np.stack(bands), BF16)
    a = np.arange(ROWB)
    same = (a[:, None] // CHUNK) == (a[None, :] // CHUNK)
    tri = np.stack([same & (a[None, :] <= a[:, None]), same & (a[None, :] >= a[:, None])]).astype(np.float32)
    return band, jnp.asarray(tri, BF16)


def _prep(up, uq, ba, pool_w, pool_scale, conv_w, gate_p, ts):
    b, lp, _ = up.shape
    nt = lp // ts
    hb = ts // HALO
    band, tri = _prep_constants()
    cqkv = 2 * DN_QK + DN_V
    main = lambda n: pl.BlockSpec((1, ts, n), lambda bi, i: (bi, i, 0))
    left = lambda n: pl.BlockSpec((1, HALO, n), lambda bi, i: (bi, jnp.maximum(i * hb - 1, 0), 0))
    right = lambda n: pl.BlockSpec((1, HALO, n), lambda bi, i: (bi, jnp.minimum((i + 1) * hb, lp // HALO - 1), 0))
    out = lambda n, dt: jax.ShapeDtypeStruct((b, lp, n), dt)
    return pl.pallas_call(
        functools.partial(_prep_kernel, ts=ts, lp=lp),
        grid=(b, nt),
        in_specs=[main(POOL_WIDTH), left(POOL_WIDTH), right(POOL_WIDTH), main(cqkv), left(cqkv), right(cqkv),
                  main(LANE), _const_spec(band.shape), _const_spec(tri.shape), _const_spec(pool_w.shape),
                  _const_spec(pool_scale.shape), _const_spec(conv_w.shape), _const_spec(gate_p.shape)],
        out_specs=[main(POOL_WIDTH), main(DN_QK), main(DN_QK), main(DN_V), main(LANE)],
        out_shape=[out(POOL_WIDTH, BF16), out(DN_QK, BF16), out(DN_QK, BF16), out(DN_V, BF16), out(LANE, F32)],
        scratch_shapes=[pltpu.VMEM((ts + 2 * HALO, POOL_WIDTH), BF16), pltpu.VMEM((cqkv // LANE, ts + 2 * HALO, LANE), F32),
                        pltpu.VMEM((ts + 2 * HALO, LANE), BF16)],
        compiler_params=_params(("parallel", "parallel")),
        name="mixer_prep",
    )(up, up, up, uq, uq, uq, ba, band, tri, pool_w, pool_scale, conv_w, gate_p)


def _pair_masks():
    r = lax.broadcasted_iota(jnp.int32, (PAIR, PAIR), 0)
    c = lax.broadcasted_iota(jnp.int32, (PAIR, PAIR), 1)
    top = (r < CHUNK) & (c < CHUNK)
    bot = (r >= CHUNK) & (c >= CHUNK)
    incl = (top & (r >= c)) | (bot & (r <= c))
    strict = (top & (r > c)) | (bot & (r < c))
    m16 = (r // 16) == (c // 16)
    m32 = (r // 32) == (c // 32)
    eye = (r == c).astype(F32)
    return incl, strict, m16, (m32 & ~m16, ~m32), eye


def _unit_tri_inverse(lmats, m16, off_masks, eye):
    ns = [-jnp.where(m16, lm, 0.0) for lm in lmats]
    ts_ = [eye + n for n in ns]
    ps = ns
    for _ in range(3):
        pbs = [p.astype(BF16) for p in ps]
        ps = [_dot(pb, pb) for pb in pbs]
        ts_ = [t + _dot(t.astype(BF16), p.astype(BF16)) for t, p in zip(ts_, ps)]
    for lo_mask in off_masks:
        tbs = [t.astype(BF16) for t in ts_]
        offs = [jnp.where(lo_mask, lm, 0.0).astype(BF16) for lm in lmats]
        mids = [_dot(tb, off).astype(BF16) for tb, off in zip(tbs, offs)]
        ts_ = [t - _dot(mid, tb) for t, mid, tb in zip(ts_, mids, tbs)]
    return ts_


def _intra_chunks(q_ref, k_ref, v_ref, gt_ref, w_ref, u_ref, qd_ref, kd_ref, at_ref, gl_ref, chunks, masks):
    incl, strict, m16, off_masks, eye = masks
    lane_row = lax.broadcasted_iota(jnp.int32, (1, PAIR), 1)
    sub_col = lax.broadcasted_iota(jnp.int32, (PAIR, 1), 0)
    nh = DN_HEADS
    inst = []
    for j in chunks:
        rows = pl.ds(pl.multiple_of(j * CHUNK, CHUNK), CHUNK)
        gt = gt_ref[0, rows, :]
        g2t = jnp.concatenate([gt, gt], axis=0).T
        for h in range(nh):
            sl = slice(h * LANE, (h + 1) * LANE)
            qc, kc, vc = q_ref[0, rows, sl], k_ref[0, rows, sl], v_ref[0, rows, sl]
            k2, q2 = jnp.concatenate([kc, kc], axis=0), jnp.concatenate([qc, qc], axis=0)
            k2f = k2.astype(F32)
            bcol = jnp.concatenate([gt[:, h:h + 1], gt[:, nh + h:nh + h + 1]], axis=0)
            gcol = jnp.concatenate([gt[:, 2 * nh + h:2 * nh + h + 1], gt[:, 3 * nh + h:3 * nh + h + 1]], axis=0)
            grow = jnp.where(lane_row < CHUNK, g2t[2 * nh + h:2 * nh + h + 1, :],
                             g2t[3 * nh + h:3 * nh + h + 1, :])
            kb2 = k2f * bcol
            inst.append(dict(j=j, h=h, rows=rows, k2=k2, q2=q2, k2f=k2f, vc=vc, bcol=bcol, gcol=gcol, grow=grow,
                             kb2=kb2))
    raws = [_dot_nt(jnp.concatenate([it["kb2"].astype(BF16), it["q2"]], axis=0), it["k2"]) for it in inst]
    decays = [jnp.exp(jnp.where(incl, it["gcol"] - it["grow"], -jnp.inf)) for it in inst]
    lmats = [jnp.where(strict, raw[:PAIR] * dec, 0.0) for raw, dec in zip(raws, decays)]
    tinv = _unit_tri_inverse(lmats, m16, off_masks, eye)
    rhss = []
    for it in inst:
        e2 = jnp.exp(it["gcol"])
        v2f = jnp.concatenate([it["vc"], it["vc"]], axis=0).astype(F32)
        rhss.append(jnp.concatenate([v2f * it["bcol"], it["kb2"] * e2], axis=1).astype(BF16))
        it["e2"] = e2
    uws = [_dot(t.astype(BF16), rhs) for t, rhs in zip(tinv, rhss)]
    for it, raw, dec, uw in zip(inst, raws, decays, uws):
        h, rows, gcol = it["h"], it["rows"], it["gcol"]
        attn = raw[PAIR:] * dec
        qd2 = it["q2"].astype(F32) * it["e2"]
        gl_col = jnp.where(sub_col < CHUNK, gcol[CHUNK - 1:CHUNK, :], gcol[CHUNK:CHUNK + 1, :])
        kd2 = it["k2f"] * jnp.exp(gl_col - gcol)
        for d in range(2):
            half = slice(d * CHUNK, (d + 1) * CHUNK)
            dst = slice((d * nh + h) * LANE, (d * nh + h + 1) * LANE)
            u_ref[0, rows, dst] = uw[half, :LANE].astype(BF16)
            w_ref[0, rows, dst] = uw[half, LANE:].astype(BF16)
            qd_ref[0, rows, dst] = qd2[half].astype(BF16)
            kd_ref[0, rows, dst] = kd2[half].astype(BF16)
            at_ref[0, rows, dst] = attn[half].astype(BF16)
            g_last = gcol[CHUNK - 1:CHUNK, :] if d == 0 else gcol[CHUNK:CHUNK + 1, :]
            gl_ref[0, it["j"], d * nh + h:d * nh + h + 1, :] = jnp.broadcast_to(jnp.exp(g_last), (1, LANE))


def _delta_intra_kernel(q_ref, k_ref, v_ref, gt_ref, w_ref, u_ref, qd_ref, kd_ref, at_ref, gl_ref, *, ts):
    masks = _pair_masks()

    ncb = ts // CHUNK
    per_body = next(n for n in (5, 4, 3, 2, 1) if ncb % n == 0)

    def body(jj, _):
        _intra_chunks(q_ref, k_ref, v_ref, gt_ref, w_ref, u_ref, qd_ref, kd_ref, at_ref, gl_ref,
                      tuple(per_body * jj + i for i in range(per_body)), masks)
        return 0

    lax.fori_loop(0, ncb // per_body, body, 0)


def _delta_intra(q, k, v, gt, ts):
    b, lp, _ = q.shape
    ncb = ts // CHUNK
    wide = 2 * DN_HEADS * LANE
    tile = lambda n: pl.BlockSpec((1, ts, n), lambda bi, i: (bi, i, 0))
    big = jax.ShapeDtypeStruct((b, lp, wide), BF16)
    return pl.pallas_call(
        functools.partial(_delta_intra_kernel, ts=ts),
        grid=(b, lp // ts),
        in_specs=[tile(DN_QK), tile(DN_QK), tile(DN_V), tile(LANE)],
        out_specs=[tile(wide)] * 5 + [pl.BlockSpec((1, ncb, 2 * DN_HEADS, LANE), lambda bi, i: (bi, i, 0, 0))],
        out_shape=[big] * 5 + [jax.ShapeDtypeStruct((b, lp // CHUNK, 2 * DN_HEADS, LANE), F32)],
        compiler_params=_params(("parallel", "parallel")),
        name="delta_intra",
    )(q, k, v, gt)


def _delta_inter_kernel(wf_ref, wb_ref, uf_ref, ub_ref, qdf_ref, qdb_ref, kdf_ref, kdb_ref, atf_ref, atb_ref,
                        glf_ref, glb_ref, of_ref, ob_ref, s_ref, *, ts, nb):
    ncb = ts // CHUNK
    nh = DN_HEADS

    @pl.when(pl.program_id(1) == 0)
    def _():
        s_ref[...] = jnp.zeros(s_ref.shape, F32)

    zero = jnp.zeros((CHUNK, LANE), BF16)
    left = lambda x: jnp.concatenate([x, zero], axis=1)
    right = lambda x: jnp.concatenate([zero, x], axis=1)

    chains = [(bb, h, slice(h * LANE, (h + 1) * LANE)) for bb in range(nb) for h in range(nh)]

    def body(j, _):
        jb = ncb - 1 - j
        rf = pl.ds(pl.multiple_of(j * CHUNK, CHUNK), CHUNK)
        rb = pl.ds(pl.multiple_of(jb * CHUNK, CHUNK), CHUNK)
        s2s = [s_ref[bb * nh + h] for bb, h, _ in chains]
        lhss = [jnp.concatenate([left(wf_ref[bb, rf, sl]), right(wb_ref[bb, rb, sl]),
                                 left(qdf_ref[bb, rf, sl]), right(qdb_ref[bb, rb, sl])], axis=0)
                for bb, _, sl in chains]
        r1s = [_dot(lhs, s2.astype(BF16)) for lhs, s2 in zip(lhss, s2s)]
        v_news = [(jnp.concatenate([uf_ref[bb, rf, sl], ub_ref[bb, rb, sl]], axis=0).astype(F32) - r1[:PAIR])
                  .astype(BF16) for (bb, _, sl), r1 in zip(chains, r1s)]
        o2s = [r1[PAIR:] + _dot(jnp.concatenate([atf_ref[bb, rf, sl], atb_ref[bb, rb, sl]], axis=0), v_new)
               for (bb, _, sl), r1, v_new in zip(chains, r1s, v_news)]
        upds = [_dot_tn(jnp.concatenate([left(kdf_ref[bb, rf, sl]), right(kdb_ref[bb, rb, sl])], axis=0), v_new)
                for (bb, _, sl), v_new in zip(chains, v_news)]
        for (bb, h, sl), s2, o2, upd in zip(chains, s2s, o2s, upds):
            of_ref[bb, rf, sl] = o2[:CHUNK].astype(BF16)
            ob_ref[bb, rb, sl] = o2[CHUNK:].astype(BF16)
            scale = jnp.concatenate([jnp.broadcast_to(glf_ref[bb, j, h:h + 1, :], (LANE, LANE)),
                                     jnp.broadcast_to(glb_ref[bb, jb, nh + h:nh + h + 1, :], (LANE, LANE))], axis=0)
            s_ref[bb * nh + h] = s2 * scale + upd
        return 0

    lax.fori_loop(0, ncb, body, 0)


def _delta_inter(w, u, qd, kd, at, gl, ts):
    b, lp, _ = w.shape
    nt = lp // ts
    ncb = ts // CHUNK
    half = DN_HEADS * LANE
    nb = 2 if b % 2 == 0 else 1
    fwd = pl.BlockSpec((nb, ts, half), lambda bi, i: (bi, i, 0))
    bwd = pl.BlockSpec((nb, ts, half), lambda bi, i: (bi, nt - 1 - i, 1))
    glf = pl.BlockSpec((nb, ncb, 2 * DN_HEADS, LANE), lambda bi, i: (bi, i, 0, 0))
    glb = pl.BlockSpec((nb, ncb, 2 * DN_HEADS, LANE), lambda bi, i: (bi, nt - 1 - i, 0, 0))
    out = jax.ShapeDtypeStruct((b, lp, half), BF16)
    return pl.pallas_call(
        functools.partial(_delta_inter_kernel, ts=ts, nb=nb),
        grid=(b // nb, nt),
        in_specs=[fwd, bwd] * 5 + [glf, glb],
        out_specs=[pl.BlockSpec((nb, ts, half), lambda bi, i: (bi, i, 0)),
                   pl.BlockSpec((nb, ts, half), lambda bi, i: (bi, nt - 1 - i, 0))],
        out_shape=[out, out],
        scratch_shapes=[pltpu.VMEM((nb * DN_HEADS, 2 * LANE, LANE), F32)],
        compiler_params=_params(("parallel", "arbitrary")),
        name="delta_inter",
    )(w, w, u, u, qd, qd, kd, kd, at, at, gl, gl)


def _mlp_tail(h1, g_ref, w1_ref, w2_ref, o_ref):
    hn = _rms_rows(h1, g_ref[...]).astype(BF16)
    acc = jnp.zeros_like(h1)
    for c in range(D_FF // FF_CHUNK):
        sl = slice(c * FF_CHUNK, (c + 1) * FF_CHUNK)
        hid = jnp.maximum(_dot(hn, w1_ref[:, sl]), 0.0)
        acc = acc + _dot((hid * hid).astype(BF16), w2_ref[sl, :])
    o_ref[...] = h1 + acc


def _out_mlp_even_kernel(yp_ref, of_ref, ob_ref, z_ref, ng_ref, wop_ref, wod_ref, h_ref, g_ref, w1_ref, w2_ref,
                         o_ref):
    ones_bf = jnp.ones((LANE, LANE), BF16)
    h1 = h_ref[...] + _dot(yp_ref[...], wop_ref[...])
    for hh in range(DN_HEADS):
        sl = slice(hh * LANE, (hh + 1) * LANE)
        o = of_ref[:, sl].astype(F32) + ob_ref[:, sl].astype(F32)
        ms = _lane_sum(o * o, ones_bf) * (1.0 / LANE)
        zf = z_ref[:, sl].astype(F32)
        y = o * lax.rsqrt(ms + EPS) * ng_ref[...] * (zf * _sigmoid(zf))
        h1 = h1 + _dot(y.astype(BF16), wod_ref[sl, :])
    _mlp_tail(h1, g_ref, w1_ref, w2_ref, o_ref)


def _out_mlp_even(yp, of, ob, z, ng, wop, wod, h2, g, w1, w2, tt):
    r = h2.shape[0]
    row = lambda n: pl.BlockSpec((tt, n), lambda i: (i, 0))
    return pl.pallas_call(
        _out_mlp_even_kernel,
        grid=(r // tt,),
        in_specs=[row(POOL_WIDTH), row(DN_V), row(DN_V), row(DN_V), _const_spec((1, LANE)), _const_spec(wop.shape),
                  _const_spec(wod.shape), row(D_MODEL), _const_spec((1, D_MODEL)), _const_spec(w1.shape),
                  _const_spec(w2.shape)],
        out_specs=row(D_MODEL),
        out_shape=jax.ShapeDtypeStruct((r, D_MODEL), F32),
        compiler_params=_params(("parallel",)),
        name="out_proj_mlp_even",
    )(yp, of, ob, z, ng, wop, wod, h2, g, w1, w2)


def _out_mlp_last_kernel(*refs, n_blk):
    a_refs, h_refs = refs[:n_blk], refs[n_blk:2 * n_blk]
    wo_ref, g_ref, w1_ref, w2_ref, o_ref = refs[2 * n_blk:]
    a = jnp.concatenate([r[0] for r in a_refs], axis=0)
    h = jnp.concatenate([r[0] for r in h_refs], axis=0)
    h1 = h + _dot(a, wo_ref[...])
    hn = _rms_rows(h1, g_ref[...]).astype(BF16)
    acc = jnp.zeros_like(h1)
    for c in range(D_FF // FF_CHUNK):
        sl = slice(c * FF_CHUNK, (c + 1) * FF_CHUNK)
        hid = jnp.maximum(_dot(hn, w1_ref[:, sl]), 0.0)
        acc = acc + _dot((hid * hid).astype(BF16), w2_ref[sl, :])
    o_ref[0] = h1 + acc


def _out_mlp_last(a3, wo, h3, g, w1, w2, tt, b0, nb):
    _, lp, _ = h3.shape
    s = lp - LANE
    n_blk = tt // LANE
    last = lp // LANE - 1

    def blk(k, n):
        return pl.BlockSpec((1, LANE, n), lambda bi, j: (b0 + bi, jnp.minimum(1 + n_blk * j + k, last), 0))

    return pl.pallas_call(
        functools.partial(_out_mlp_last_kernel, n_blk=n_blk),
        grid=(nb, -(-s // tt)),
        in_specs=[blk(k, a3.shape[2]) for k in range(n_blk)] + [blk(k, D_MODEL) for k in range(n_blk)]
        + [_const_spec(wo.shape), _const_spec((1, D_MODEL)), _const_spec(w1.shape), _const_spec(w2.shape)],
        out_specs=pl.BlockSpec((1, tt, D_MODEL), lambda bi, j: (bi, j, 0)),
        out_shape=jax.ShapeDtypeStruct((nb, s, D_MODEL), F32),
        compiler_params=_params(("parallel", "parallel")),
        name="out_proj_mlp_last",
    )(*([a3] * n_blk), *([h3] * n_blk), wo, g, w1, w2)


def _in_odd_kernel(h_ref, g_ref, wq_ref, wk_ref, wv_ref, qn_ref, kn_ref, cos_ref, sin_ref, q_ref, k_ref, v_ref):
    xn = _rms_rows(h_ref[...], g_ref[...]).astype(BF16)
    ones_bf = jnp.ones((LANE, LANE), BF16)
    cos = cos_ref[...]
    sin = sin_ref[...]

    def norm_rope(x, gain, scale):
        ms = _lane_sum(x * x, ones_bf) * (1.0 / ATT_HD)
        y = x * lax.rsqrt(ms + EPS) * gain
        return (y * cos + pltpu.roll(y, ATT_HD // 2, axis=1) * sin) * scale

    uq = _dot(xn, wq_ref[...])
    for hh in range(ATT_HEADS):
        sl = slice(hh * ATT_HD, (hh + 1) * ATT_HD)
        q_ref[:, sl] = norm_rope(uq[:, sl], qn_ref[...], ATT_HD ** -0.5 * LOG2E).astype(BF16)
    uk = _dot(xn, wk_ref[...])
    for hh in range(ATT_KV_HEADS):
        sl = slice(hh * ATT_HD, (hh + 1) * ATT_HD)
        k_ref[:, sl] = norm_rope(uk[:, sl], kn_ref[...], 1.0).astype(BF16)
    v_ref[...] = _dot(xn, wv_ref[...]).astype(BF16)


def _in_odd(h2, g, wq, wk, wv, qn, kn, cos, sin, tt):
    r = h2.shape[0]
    ntb = cos.shape[0] // tt
    row = lambda n: pl.BlockSpec((tt, n), lambda i: (i, 0))
    tab = pl.BlockSpec((tt, ATT_HD), lambda i: (i % ntb, 0))
    nq, nk = ATT_HEADS * ATT_HD, ATT_KV_HEADS * ATT_HD
    return pl.pallas_call(
        _in_odd_kernel,
        grid=(r // tt,),
        in_specs=[row(D_MODEL), _const_spec((1, D_MODEL)), _const_spec(wq.shape), _const_spec(wk.shape),
                  _const_spec(wv.shape), _const_spec((1, ATT_HD)), _const_spec((1, ATT_HD)), tab, tab],
        out_specs=[row(nq), row(nk), row(nk)],
        out_shape=[jax.ShapeDtypeStruct((r, nq), BF16), jax.ShapeDtypeStruct((r, nk), BF16),
                   jax.ShapeDtypeStruct((r, nk), BF16)],
        compiler_params=_params(("parallel",)),
        name="in_proj_odd",
    )(h2, g, wq, wk, wv, qn, kn, cos, sin)


def _attn_kernel(q_ref, k_ref, v_ref, o_ref, kp_ref, vaug_ref, kmax_ref, qs_ref, b_ref, acc_ref, *, tqb, tk, lp):
    sub = LANE
    n_sub = tqb // sub
    lpk = kp_ref.shape[0]
    n_kc = lpk // tk
    rows_q = ATT_GROUP * sub
    ones_bf = jnp.ones((LANE, LANE), BF16)

    @pl.when(pl.program_id(2) == 0)
    def _():
        def fill(c, kmax):
            rows = pl.ds(pl.multiple_of(c * tqb, tqb), tqb)
            valid = (lax.broadcasted_iota(jnp.int32, (tqb, LANE), 0) + c * tqb) >= FRONT
            vaug_ref[rows, :ATT_HD] = jnp.where(valid, v_ref[0, rows, :], jnp.zeros((tqb, ATT_HD), BF16))
            vaug_ref[rows, ATT_HD:] = jnp.where(valid, 1.0, 0.0).astype(BF16)
            kc = k_ref[0, rows, :]
            kp_ref[rows, :] = kc
            kf = kc.astype(F32)
            n2 = jnp.where(valid, _lane_sum(kf * kf, ones_bf), 0.0)
            return jnp.maximum(kmax, jnp.max(n2, axis=0, keepdims=True))

        kmax = lax.fori_loop(0, lp // tqb, fill, jnp.zeros((1, LANE), F32))
        kmax_ref[...] = jnp.broadcast_to(kmax, kmax_ref.shape)
        if lpk > lp:
            kp_ref[lp:, :] = jnp.zeros((lpk - lp, ATT_HD), BF16)
            vaug_ref[lp:, :] = jnp.zeros((lpk - lp, 2 * ATT_HD), BF16)

    kmax2 = kmax_ref[0:1, :]
    for si in range(n_sub):
        q4 = q_ref[0, si * sub:(si + 1) * sub, :]
        qs = jnp.concatenate([q4[:, g * ATT_HD:(g + 1) * ATT_HD] for g in range(ATT_GROUP)], axis=0)
        qs_ref[si] = qs
        qf = qs.astype(F32)
        b_ref[si] = jnp.sqrt(_lane_sum(qf * qf, ones_bf) * kmax2) * (1.0 + 2.0 ** -10)
        acc_ref[si] = jnp.zeros((rows_q, 2 * ATT_HD), F32)

    def fast(c, _):
        rows = pl.ds(pl.multiple_of(c * tk, tk), tk)
        kc = kp_ref[rows, :]
        va = vaug_ref[rows, :]
        for si in range(n_sub):
            s = _dot_nt(qs_ref[si], kc)
            p = jnp.exp2(s - pltpu.repeat(b_ref[si], tk // LANE, axis=1)).astype(BF16)
            acc_ref[si] += _dot(p, va)
        return 0

    lax.fori_loop(0, n_kc, fast, 0, unroll=True)

    def write(si, acc):
        out = (acc[:, :ATT_HD] / acc[:, ATT_HD:]).astype(BF16)
        start = si * sub if isinstance(si, int) else pl.multiple_of(si * sub, sub)
        for g in range(ATT_GROUP):
            o_ref[0, pl.ds(start, sub), g * ATT_HD:(g + 1) * ATT_HD] = out[g * sub:(g + 1) * sub]

    l_min = None
    for si in range(n_sub):
        acc = acc_ref[si]
        write(si, acc)
        m = jnp.min(acc[:, ATT_HD:])
        l_min = m if l_min is None else jnp.minimum(l_min, m)

    @pl.when(jnp.logical_not(l_min >= SOFTMAX_MIN_DENOM))
    def _():
        col = lax.broadcasted_iota(jnp.int32, (rows_q, tk), 1)

        def redo(si, _):
            qs = qs_ref[si]

            def step(c, carry):
                m, acc = carry
                rows = pl.ds(pl.multiple_of(c * tk, tk), tk)
                key = col + c * tk
                s = jnp.where((key >= FRONT) & (key < lp), _dot_nt(qs, kp_ref[rows, :]), -jnp.inf)
                m_new = jnp.maximum(m, jnp.max(s, axis=1, keepdims=True))
                p = jnp.exp2(s - m_new).astype(BF16)
                return m_new, jnp.exp2(m - m_new) * acc + _dot(p, vaug_ref[rows, :])

            init = (jnp.full((rows_q, 1), -jnp.inf, F32), jnp.zeros((rows_q, 2 * ATT_HD), F32))
            _, acc = lax.fori_loop(0, n_kc, step, init)
            write(si, acc)
            return 0

        lax.fori_loop(0, n_sub, redo, 0)


def _attention(q, k, v, tqb):
    b, lp, _ = q.shape
    gw = ATT_GROUP * ATT_HD
    n_sub = tqb // LANE
    rows_q = ATT_GROUP * LANE
    qspec = pl.BlockSpec((1, tqb, gw), lambda bi, hi, i: (bi, i, hi))
    kvspec = pl.BlockSpec((1, lp, ATT_HD), lambda bi, hi, i: (bi, 0, hi))
    tk = min(3 * MXU_TILE, -(-lp // MXU_TILE) * MXU_TILE)
    lpk = -(-lp // tk) * tk
    return pl.pallas_call(
        functools.partial(_attn_kernel, tqb=tqb, tk=tk, lp=lp),
        grid=(b, ATT_KV_HEADS, lp // tqb),
        in_specs=[qspec, kvspec, kvspec],
        out_specs=qspec,
        out_shape=jax.ShapeDtypeStruct((b, lp, ATT_HEADS * ATT_HD), BF16),
        scratch_shapes=[pltpu.VMEM((lpk, ATT_HD), BF16), pltpu.VMEM((lpk, 2 * ATT_HD), BF16),
                        pltpu.VMEM((8, LANE), F32),
                        pltpu.VMEM((n_sub, rows_q, ATT_HD), BF16), pltpu.VMEM((n_sub, rows_q, LANE), F32),
                        pltpu.VMEM((n_sub, rows_q, 2 * ATT_HD), F32)],
        compiler_params=_params(("parallel", "parallel", "arbitrary")),
        name="gqa_attention",
    )(q, k, v)


def _rope_tables(lp):
    s = lp - LANE
    pos = np.arange(s)
    freqs = jnp.asarray(ROPE_THETA, F32) ** (-(jnp.arange(ATT_HD // 4, dtype=F32) / (ATT_HD // 4)))
    row = jnp.asarray(pos // GRID_W, F32)
    colp = jnp.asarray(pos % GRID_W, F32)
    ang = jnp.concatenate([row[:, None] * freqs, colp[:, None] * freqs], axis=-1)
    ang = jnp.concatenate([jnp.zeros((LANE, ATT_HD // 2), F32), ang], axis=0)
    c, sn = jnp.cos(ang), jnp.sin(ang)
    return jnp.concatenate([c, c], axis=-1), jnp.concatenate([-sn, sn], axis=-1)


def _trunk(xa, xb, meta_tokens, mix_norm, mlp_norm, w_in_even, pool_w, pool_scale, conv_qkv, a_log, dt_bias,
           delta_norm, w_out_even, w_in_odd, q_norm, k_norm, w_out_odd, w_mlp_in, w_mlp_out):
    groups = (xa.shape[0], xb.shape[0])
    b = sum(groups)
    s = xa.shape[1]
    lp = s + LANE
    tt = _token_tile(lp)
    r = b * lp

    o1 = POOL_WIDTH
    o2 = o1 + 2 * DN_QK + DN_V
    o3 = o2 + DN_V
    w_in = w_in_even[0]
    wb = jnp.zeros((D_MODEL, LANE), F32).at[:, :4 * DN_HEADS].set(w_in[:, o3:])
    meta_pad = jnp.concatenate([jnp.zeros((FRONT, D_MODEL), F32), meta_tokens], axis=0)
    h2, up, uq, z, ba = _in_even(xa, xb, meta_pad, mix_norm[0][None], w_in[:, :o1].astype(BF16),
                                 w_in[:, o1:o2].astype(BF16), w_in[:, o2:o3].astype(BF16), wb.astype(BF16), tt)
    gate_p = jnp.zeros((8, LANE), F32)
    gate_p = gate_p.at[0, 2 * DN_HEADS:4 * DN_HEADS].set(-jnp.exp(a_log[0].reshape(-1)))
    gate_p = gate_p.at[1, 2 * DN_HEADS:4 * DN_HEADS].set(dt_bias[0].reshape(-1))
    conv_w = jnp.zeros((8, 2 * DN_QK + DN_V), F32).at[:DN_CONV].set(conv_qkv[0])
    seq = lambda a: a.reshape(b, lp, a.shape[-1])
    y_pool, q, k, v, gt = _prep(seq(up), seq(uq), seq(ba), pool_w[0].astype(BF16), pool_scale[0][None], conv_w,
                                gate_p, tt)
    w, u, qd, kd, at, gl = _delta_intra(q, k, v, gt, tt)
    o_f, o_b = _delta_inter(w, u, qd, kd, at, gl, tt)
    wo = w_out_even[0].astype(BF16)
    flat = lambda a: a.reshape(r, a.shape[-1])
    h2 = _out_mlp_even(flat(y_pool), flat(o_f), flat(o_b), z, delta_norm[0][None], wo[:POOL_WIDTH], wo[POOL_WIDTH:],
                       h2, mlp_norm[0][None], w_mlp_in[0].astype(BF16), w_mlp_out[0].astype(BF16), tt)

    nq, nk = ATT_HEADS * ATT_HD, ATT_KV_HEADS * ATT_HD
    w_in = w_in_odd[0].astype(BF16)
    cos, sin = _rope_tables(lp)
    qa, ka, va = _in_odd(h2, mix_norm[1][None], w_in[:, :nq], w_in[:, nq:nq + nk], w_in[:, nq + nk:],
                         q_norm[0][None], k_norm[0][None], cos, sin, tt)
    att = _attention(seq(qa), seq(ka), seq(va), tt)
    outs, b0 = [], 0
    for nb in groups:
        outs.append(_out_mlp_last(att, w_out_odd[0].astype(BF16), h2.reshape(b, lp, D_MODEL), mlp_norm[1][None],
                                  w_mlp_in[1].astype(BF16), w_mlp_out[1].astype(BF16), tt, b0, nb))
        b0 += nb
    return tuple(outs)


def kernel(x_prompt, x_sample, meta_tokens, mix_norm, mlp_norm, w_in_even, pool_w, pool_scale, conv_qkv, a_log,
           dt_bias, delta_norm, w_out_even, w_in_odd, q_norm, k_norm, w_out_odd, w_mlp_in, w_mlp_out):
    return _trunk(x_prompt, x_sample, meta_tokens, mix_norm, mlp_norm, w_in_even, pool_w, pool_scale, conv_qkv, a_log,
                  dt_bias, delta_norm, w_out_even, w_in_odd, q_norm, k_norm, w_out_odd, w_mlp_in, w_mlp_out)
```

```python
import functools
import math

import jax
import jax.numpy as jnp
import numpy as np
from jax import lax
from jax.experimental import pallas as pl
from jax.experimental.pallas import tpu as pltpu

F32 = jnp.float32
BF16 = jnp.bfloat16

D_MODEL = 1024
N_META = 16
LANE = 128
MXU_TILE = 256
HALO = 16
FRONT = LANE - N_META
GRID_W = 64
EPS = 1e-6
POOL_WINDOWS = (2, 4, 8, 16)
POOL_WIDTH = 512
DN_HEADS = 4
DN_CONV = 7
CHUNK = 64
PAIR = 2 * CHUNK
ROWB = 128
DN_QK = 512
DN_V = 512
ATT_HD = 128
ATT_HEADS = 8
ATT_KV_HEADS = 2
ATT_GROUP = 4
ROPE_THETA = 10000.0
D_FF = 4096
FF_CHUNK = 1024
VMEM_LIMIT = 56 * 1024 * 1024
LOG2E = math.log2(math.e)
SOFTMAX_MIN_DENOM = 1e-26


def _token_tile(lp):
    for t in (640, 512, 384, 256, 128):
        if lp % t == 0:
            return t
    raise ValueError(lp)


def _params(sem):
    return pltpu.CompilerParams(dimension_semantics=sem, vmem_limit_bytes=VMEM_LIMIT)


def _const_spec(shape):
    nd = len(shape)
    return pl.BlockSpec(shape, lambda *_: (0,) * nd, pipeline_mode=pl.Buffered(1))


def _dot(a, b):
    return jnp.dot(a, b, preferred_element_type=F32)


def _dot_nt(a, b):
    return lax.dot_general(a, b, (((1,), (1,)), ((), ())), preferred_element_type=F32)


def _dot_tn(a, b):
    return lax.dot_general(a, b, (((0,), (0,)), ((), ())), preferred_element_type=F32)


def _split2(x):
    hi = x.astype(BF16)
    lo = (x - hi.astype(F32)).astype(BF16)
    return hi, lo


def _split3(x):
    hi = x.astype(BF16)
    r = x - hi.astype(F32)
    mid = r.astype(BF16)
    lo = (r - mid.astype(F32)).astype(BF16)
    return hi, mid, lo


def _lane_sum(x, ones_bf):
    hi, lo = _split2(x)
    return _dot(hi, ones_bf) + _dot(lo, ones_bf)


def _sigmoid(x):
    return 1.0 / (1.0 + jnp.exp(-x))


def _rms_rows(x, g):
    ms = jnp.mean(x * x, axis=-1, keepdims=True)
    return x * lax.rsqrt(ms + EPS) * g


def _in_even_kernel(*refs, n_blk, nt, nb0):
    xa_refs, xb_refs = refs[:n_blk], refs[n_blk:2 * n_blk]
    mp_ref, g_ref, wp_ref, wq_ref, wz_ref, wb_ref, h_ref, up_ref, uq_ref, z_ref, ba_ref = refs[2 * n_blk:]
    i = pl.program_id(0)
    first_group = (i // nt) < nb0
    pieces = [jnp.where(first_group, xa[0], xb[0]) for xa, xb in zip(xa_refs, xb_refs)]
    pieces[0] = jnp.where((i % nt) == 0, mp_ref[...], pieces[0])
    h = jnp.concatenate(pieces, axis=0)
    h_ref[...] = h
    xn = _rms_rows(h, g_ref[...]).astype(BF16)
    up_ref[...] = _dot(xn, wp_ref[...]).astype(BF16)
    uq_ref[...] = _dot(xn, wq_ref[...]).astype(BF16)
    z_ref[...] = _dot(xn, wz_ref[...]).astype(BF16)
    ba_ref[...] = _dot(xn, wb_ref[...])


def _in_even(xa, xb, meta_pad, g, wp, wq, wz, wb, tt):
    nb0, s, _ = xa.shape
    nb1 = xb.shape[0]
    lp = s + LANE
    nt = lp // tt
    n_blk = tt // LANE
    r = (nb0 + nb1) * lp

    def xspec(k, g0, nbg):
        def index(i):
            bi, j = i // nt, i % nt
            mine = (bi >= g0) & (bi < g0 + nbg)
            return (jnp.clip(bi - g0, 0, nbg - 1), jnp.where(mine, jnp.maximum(n_blk * j + k - 1, 0), 0), 0)
        return pl.BlockSpec((1, LANE, D_MODEL), index)

    row = lambda n: pl.BlockSpec((tt, n), lambda i: (i, 0))
    return pl.pallas_call(
        functools.partial(_in_even_kernel, n_blk=n_blk, nt=nt, nb0=nb0),
        grid=(r // tt,),
        in_specs=[xspec(k, 0, nb0) for k in range(n_blk)] + [xspec(k, nb0, nb1) for k in range(n_blk)]
        + [_const_spec((LANE, D_MODEL)), _const_spec((1, D_MODEL)), _const_spec(wp.shape), _const_spec(wq.shape),
           _const_spec(wz.shape), _const_spec(wb.shape)],
        out_specs=[row(D_MODEL), row(POOL_WIDTH), row(2 * DN_QK + DN_V), row(DN_V), row(LANE)],
        out_shape=[jax.ShapeDtypeStruct((r, D_MODEL), F32), jax.ShapeDtypeStruct((r, POOL_WIDTH), BF16),
                   jax.ShapeDtypeStruct((r, 2 * DN_QK + DN_V), BF16), jax.ShapeDtypeStruct((r, DN_V), BF16),
                   jax.ShapeDtypeStruct((r, LANE), F32)],
        compiler_params=_params(("parallel",)),
        name="in_proj_even",
    )(*([xa] * n_blk), *([xb] * n_blk), meta_pad, g, wp, wq, wz, wb)


def _prep_kernel(up_ref, upl_ref, upr_ref, uq_ref, uql_ref, uqr_ref, ba_ref, band_ref, tri_ref, pw_ref, ps_ref,
                 cw_ref, gp_ref, yp_ref, q_ref, k_ref, v_ref, gt_ref, extp_ref, extq_ref, vb_ref, *, ts, lp):
    i = pl.program_id(1)
    row0 = i * ts
    next_ = ROWB + 2 * HALO
    rows_ext = lax.broadcasted_iota(jnp.int32, (ts + 2 * HALO, LANE), 0) + (row0 - HALO)
    valid_ext = (rows_ext >= FRONT) & (rows_ext < lp)
    ones_bf = jnp.ones((LANE, LANE), BF16)
    lane = lax.broadcasted_iota(jnp.int32, (ROWB, LANE), 1)
    n_blk = (2 * DN_QK + DN_V) // LANE

    vb_ref[...] = jnp.where(valid_ext, 1.0, 0.0).astype(BF16)
    for gi in range(len(POOL_WINDOWS)):
        sl = slice(gi * LANE, (gi + 1) * LANE)
        eg = jnp.concatenate([upl_ref[0, :, sl], up_ref[0, :, sl], upr_ref[0, :, sl]], axis=0)
        extp_ref[:, sl] = jnp.where(valid_ext, eg, jnp.zeros_like(eg))
    for cb in range(n_blk):
        sl = slice(cb * LANE, (cb + 1) * LANE)
        e = jnp.concatenate([uql_ref[0, :, sl], uq_ref[0, :, sl], uqr_ref[0, :, sl]], axis=0).astype(F32)
        extq_ref[cb] = jnp.where(valid_ext, e, 0.0)

    for rb in range(ts // ROWB):
        r0 = rb * ROWB
        rows = slice(r0, r0 + ROWB)
        valid_main = valid_ext[HALO + r0:HALO + r0 + ROWB]

        vb = vb_ref[r0:r0 + next_, :]
        scs = [_dot(band_ref[gi], jnp.concatenate([extp_ref[r0:r0 + next_, gi * LANE:(gi + 1) * LANE], vb], axis=1))
               for gi in range(len(POOL_WINDOWS))]
        ds_ = [(sc[:, :LANE] / jnp.maximum(sc[:, LANE:], 1.0)
                - up_ref[0, rows, gi * LANE:(gi + 1) * LANE].astype(F32)).astype(BF16) for gi, sc in enumerate(scs)]
        for gi, d in enumerate(ds_):
            sl = slice(gi * LANE, (gi + 1) * LANE)
            y = _dot(d, pw_ref[gi]) * ps_ref[:, sl]
            yp_ref[0, rows, sl] = jnp.where(valid_main, y, 0.0).astype(BF16)

        xs = []
        for cb in range(n_blk):
            sl = slice(cb * LANE, (cb + 1) * LANE)
            acc = jnp.zeros((ROWB, LANE), F32)
            for j in range(DN_CONV):
                acc = acc + cw_ref[j:j + 1, sl] * extq_ref[cb, pl.ds(r0 + HALO - DN_CONV // 2 + j, ROWB), :]
            xs.append(jnp.where(valid_main, acc * _sigmoid(acc), 0.0))
        sss = [_lane_sum(x * x, ones_bf) for x in xs[:2 * DN_HEADS]]
        for cb, x in enumerate(xs):
            hsl = slice((cb % DN_HEADS) * LANE, (cb % DN_HEADS + 1) * LANE)
            if cb < DN_HEADS:
                q_ref[0, rows, hsl] = (x * lax.rsqrt(sss[cb] + EPS) * (LANE ** -0.5)).astype(BF16)
            elif cb < 2 * DN_HEADS:
                k_ref[0, rows, hsl] = (x * lax.rsqrt(sss[cb] + EPS)).astype(BF16)
            else:
                v_ref[0, rows, hsl] = x.astype(BF16)

        ba = ba_ref[0, rows, :]
        beta = jnp.where(valid_main, _sigmoid(ba), 0.0)
        xg = ba + gp_ref[1:2, :]
        softplus = jnp.maximum(xg, 0.0) + jnp.log1p(jnp.exp(-jnp.abs(xg)))
        g = jnp.where(valid_main, gp_ref[0:1, :] * softplus, 0.0)
        g3 = _split3(g)
        gcf = _dot(tri_ref[0], g3[0]) + _dot(tri_ref[0], g3[1]) + _dot(tri_ref[0], g3[2])
        gcb = _dot(tri_ref[1], g3[0]) + _dot(tri_ref[1], g3[1]) + _dot(tri_ref[1], g3[2])
        gt_ref[0, rows, :] = jnp.where(lane < 2 * DN_HEADS, beta,
                                       jnp.where(lane < 3 * DN_HEADS, gcf, jnp.where(lane < 4 * DN_HEADS, gcb, 0.0)))


def _prep_constants():
    t = np.arange(ROWB)[:, None]
    j = np.arange(ROWB + 2 * HALO)[None, :] - HALO
    bands = []
    for w in POOL_WINDOWS:
        bands.append(((j >= t - w // 2) & (j <= t + (w - 1 - w // 2))).astype(np.float32))
    band = jnp.asarray(np.stack(bands), BF16)
    a = np.arange(ROWB)
    same = (a[:, None] // CHUNK) == (a[None, :] // CHUNK)
    tri = np.stack([same & (a[None, :] <= a[:, None]), same & (a[None, :] >= a[:, None])]).astype(np.float32)
    return band, jnp.asarray(tri, BF16)


def _prep(up, uq, ba, pool_w, pool_scale, conv_w, gate_p, ts):
    b, lp, _ = up.shape
    nt = lp // ts
    hb = ts // HALO
    band, tri = _prep_constants()
    cqkv = 2 * DN_QK + DN_V
    main = lambda n: pl.BlockSpec((1, ts, n), lambda bi, i: (bi, i, 0))
    left = lambda n: pl.BlockSpec((1, HALO, n), lambda bi, i: (bi, jnp.maximum(i * hb - 1, 0), 0))
    right = lambda n: pl.BlockSpec((1, HALO, n), lambda bi, i: (bi, jnp.minimum((i + 1) * hb, lp // HALO - 1), 0))
    out = lambda n, dt: jax.ShapeDtypeStruct((b, lp, n), dt)
    return pl.pallas_call(
        functools.partial(_prep_kernel, ts=ts, lp=lp),
        grid=(b, nt),
        in_specs=[main(POOL_WIDTH), left(POOL_WIDTH), right(POOL_WIDTH), main(cqkv), left(cqkv), right(cqkv),
                  main(LANE), _const_spec(band.shape), _const_spec(tri.shape), _const_spec(pool_w.shape),
                  _const_spec(pool_scale.shape), _const_spec(conv_w.shape), _const_spec(gate_p.shape)],
        out_specs=[main(POOL_WIDTH), main(DN_QK), main(DN_QK), main(DN_V), main(LANE)],
        out_shape=[out(POOL_WIDTH, BF16), out(DN_QK, BF16), out(DN_QK, BF16), out(DN_V, BF16), out(LANE, F32)],
        scratch_shapes=[pltpu.VMEM((ts + 2 * HALO, POOL_WIDTH), BF16), pltpu.VMEM((cqkv // LANE, ts + 2 * HALO, LANE), F32),
                        pltpu.VMEM((ts + 2 * HALO, LANE), BF16)],
        compiler_params=_params(("parallel", "parallel")),
        name="mixer_prep",
    )(up, up, up, uq, uq, uq, ba, band, tri, pool_w, pool_scale, conv_w, gate_p)


def _pair_masks():
    r = lax.broadcasted_iota(jnp.int32, (PAIR, PAIR), 0)
    c = lax.broadcasted_iota(jnp.int32, (PAIR, PAIR), 1)
    top = (r < CHUNK) & (c < CHUNK)
    bot = (r >= CHUNK) & (c >= CHUNK)
    incl = (top & (r >= c)) | (bot & (r <= c))
    strict = (top & (r > c)) | (bot & (r < c))
    m16 = (r // 16) == (c // 16)
    m32 = (r // 32) == (c // 32)
    eye = (r == c).astype(F32)
    return incl, strict, m16, (m32 & ~m16, ~m32), eye


def _unit_tri_inverse(lmats, m16, off_masks, eye):
    ns = [-jnp.where(m16, lm, 0.0) for lm in lmats]
    ts_ = [eye + n for n in ns]
    ps = ns
    for _ in range(3):
        pbs = [p.astype(BF16) for p in ps]
        ps = [_dot(pb, pb) for pb in pbs]
        ts_ = [t + _dot(t.astype(BF16), p.astype(BF16)) for t, p in zip(ts_, ps)]
    for lo_mask in off_masks:
        tbs = [t.astype(BF16) for t in ts_]
        offs = [jnp.where(lo_mask, lm, 0.0).astype(BF16) for lm in lmats]
        mids = [_dot(tb, off).astype(BF16) for tb, off in zip(tbs, offs)]
        ts_ = [t - _dot(mid, tb) for t, mid, tb in zip(ts_, mids, tbs)]
    return ts_


def _intra_chunks(q_ref, k_ref, v_ref, gt_ref, w_ref, u_ref, qd_ref, kd_ref, at_ref, gl_ref, chunks, masks):
    incl, strict, m16, off_masks, eye = masks
    lane_row = lax.broadcasted_iota(jnp.int32, (1, PAIR), 1)
    sub_col = lax.broadcasted_iota(jnp.int32, (PAIR, 1), 0)
    nh = DN_HEADS
    inst = []
    for j in chunks:
        rows = pl.ds(pl.multiple_of(j * CHUNK, CHUNK), CHUNK)
        gt = gt_ref[0, rows, :]
        g2t = jnp.concatenate([gt, gt], axis=0).T
        for h in range(nh):
            sl = slice(h * LANE, (h + 1) * LANE)
            qc, kc, vc = q_ref[0, rows, sl], k_ref[0, rows, sl], v_ref[0, rows, sl]
            k2, q2 = jnp.concatenate([kc, kc], axis=0), jnp.concatenate([qc, qc], axis=0)
            k2f = k2.astype(F32)
            bcol = jnp.concatenate([gt[:, h:h + 1], gt[:, nh + h:nh + h + 1]], axis=0)
            gcol = jnp.concatenate([gt[:, 2 * nh + h:2 * nh + h + 1], gt[:, 3 * nh + h:3 * nh + h + 1]], axis=0)
            grow = jnp.where(lane_row < CHUNK, g2t[2 * nh + h:2 * nh + h + 1, :],
                             g2t[3 * nh + h:3 * nh + h + 1, :])
            kb2 = k2f * bcol
            inst.append(dict(j=j, h=h, rows=rows, k2=k2, q2=q2, k2f=k2f, vc=vc, bcol=bcol, gcol=gcol, grow=grow,
                             kb2=kb2))
    raws = [_dot_nt(jnp.concatenate([it["kb2"].astype(BF16), it["q2"]], axis=0), it["k2"]) for it in inst]
    decays = [jnp.exp(jnp.where(incl, it["gcol"] - it["grow"], -jnp.inf)) for it in inst]
    lmats = [jnp.where(strict, raw[:PAIR] * dec, 0.0) for raw, dec in zip(raws, decays)]
    tinv = _unit_tri_inverse(lmats, m16, off_masks, eye)
    rhss = []
    for it in inst:
        e2 = jnp.exp(it["gcol"])
        v2f = jnp.concatenate([it["vc"], it["vc"]], axis=0).astype(F32)
        rhss.append(jnp.concatenate([v2f * it["bcol"], it["kb2"] * e2], axis=1).astype(BF16))
        it["e2"] = e2
    uws = [_dot(t.astype(BF16), rhs) for t, rhs in zip(tinv, rhss)]
    for it, raw, dec, uw in zip(inst, raws, decays, uws):
        h, rows, gcol = it["h"], it["rows"], it["gcol"]
        attn = raw[PAIR:] * dec
        qd2 = it["q2"].astype(F32) * it["e2"]
        gl_col = jnp.where(sub_col < CHUNK, gcol[CHUNK - 1:CHUNK, :], gcol[CHUNK:CHUNK + 1, :])
        kd2 = it["k2f"] * jnp.exp(gl_col - gcol)
        for d in range(2):
            half = slice(d * CHUNK, (d + 1) * CHUNK)
            dst = slice((d * nh + h) * LANE, (d * nh + h + 1) * LANE)
            u_ref[0, rows, dst] = uw[half, :LANE].astype(BF16)
            w_ref[0, rows, dst] = uw[half, LANE:].astype(BF16)
            qd_ref[0, rows, dst] = qd2[half].astype(BF16)
            kd_ref[0, rows, dst] = kd2[half].astype(BF16)
            at_ref[0, rows, dst] = attn[half].astype(BF16)
            g_last = gcol[CHUNK - 1:CHUNK, :] if d == 0 else gcol[CHUNK:CHUNK + 1, :]
            gl_ref[0, it["j"], d * nh + h:d * nh + h + 1, :] = jnp.broadcast_to(jnp.exp(g_last), (1, LANE))


def _delta_intra_kernel(q_ref, k_ref, v_ref, gt_ref, w_ref, u_ref, qd_ref, kd_ref, at_ref, gl_ref, *, ts):
    masks = _pair_masks()

    ncb = ts // CHUNK
    per_body = next(n for n in (5, 4, 3, 2, 1) if ncb % n == 0)

    def body(jj, _):
        _intra_chunks(q_ref, k_ref, v_ref, gt_ref, w_ref, u_ref, qd_ref, kd_ref, at_ref, gl_ref,
                      tuple(per_body * jj + i for i in range(per_body)), masks)
        return 0

    lax.fori_loop(0, ncb // per_body, body, 0)


def _delta_intra(q, k, v, gt, ts):
    b, lp, _ = q.shape
    ncb = ts // CHUNK
    wide = 2 * DN_HEADS * LANE
    tile = lambda n: pl.BlockSpec((1, ts, n), lambda bi, i: (bi, i, 0))
    big = jax.ShapeDtypeStruct((b, lp, wide), BF16)
    return pl.pallas_call(
        functools.partial(_delta_intra_kernel, ts=ts),
        grid=(b, lp // ts),
        in_specs=[tile(DN_QK), tile(DN_QK), tile(DN_V), tile(LANE)],
        out_specs=[tile(wide)] * 5 + [pl.BlockSpec((1, ncb, 2 * DN_HEADS, LANE), lambda bi, i: (bi, i, 0, 0))],
        out_shape=[big] * 5 + [jax.ShapeDtypeStruct((b, lp // CHUNK, 2 * DN_HEADS, LANE), F32)],
        compiler_params=_params(("parallel", "parallel")),
        name="delta_intra",
    )(q, k, v, gt)


def _delta_inter_kernel(wf_ref, wb_ref, uf_ref, ub_ref, qdf_ref, qdb_ref, kdf_ref, kdb_ref, atf_ref, atb_ref,
                        glf_ref, glb_ref, of_ref, ob_ref, s_ref, *, ts, nb):
    ncb = ts // CHUNK
    nh = DN_HEADS

    @pl.when(pl.program_id(1) == 0)
    def _():
        s_ref[...] = jnp.zeros(s_ref.shape, F32)

    zero = jnp.zeros((CHUNK, LANE), BF16)
    left = lambda x: jnp.concatenate([x, zero], axis=1)
    right = lambda x: jnp.concatenate([zero, x], axis=1)

    chains = [(bb, h, slice(h * LANE, (h + 1) * LANE)) for bb in range(nb) for h in range(nh)]

    def body(j, _):
        jb = ncb - 1 - j
        rf = pl.ds(pl.multiple_of(j * CHUNK, CHUNK), CHUNK)
        rb = pl.ds(pl.multiple_of(jb * CHUNK, CHUNK), CHUNK)
        s2s = [s_ref[bb * nh + h] for bb, h, _ in chains]
        lhss = [jnp.concatenate([left(wf_ref[bb, rf, sl]), right(wb_ref[bb, rb, sl]),
                                 left(qdf_ref[bb, rf, sl]), right(qdb_ref[bb, rb, sl])], axis=0)
                for bb, _, sl in chains]
        r1s = [_dot(lhs, s2.astype(BF16)) for lhs, s2 in zip(lhss, s2s)]
        v_news = [(jnp.concatenate([uf_ref[bb, rf, sl], ub_ref[bb, rb, sl]], axis=0).astype(F32) - r1[:PAIR])
                  .astype(BF16) for (bb, _, sl), r1 in zip(chains, r1s)]
        o2s = [r1[PAIR:] + _dot(jnp.concatenate([atf_ref[bb, rf, sl], atb_ref[bb, rb, sl]], axis=0), v_new)
               for (bb, _, sl), r1, v_new in zip(chains, r1s, v_news)]
        upds = [_dot_tn(jnp.concatenate([left(kdf_ref[bb, rf, sl]), right(kdb_ref[bb, rb, sl])], axis=0), v_new)
                for (bb, _, sl), v_new in zip(chains, v_news)]
        for (bb, h, sl), s2, o2, upd in zip(chains, s2s, o2s, upds):
            of_ref[bb, rf, sl] = o2[:CHUNK].astype(BF16)
            ob_ref[bb, rb, sl] = o2[CHUNK:].astype(BF16)
            scale = jnp.concatenate([jnp.broadcast_to(glf_ref[bb, j, h:h + 1, :], (LANE, LANE)),
                                     jnp.broadcast_to(glb_ref[bb, jb, nh + h:nh + h + 1, :], (LANE, LANE))], axis=0)
            s_ref[bb * nh + h] = s2 * scale + upd
        return 0

    lax.fori_loop(0, ncb, body, 0)


def _delta_inter(w, u, qd, kd, at, gl, ts):
    b, lp, _ = w.shape
    nt = lp // ts
    ncb = ts // CHUNK
    half = DN_HEADS * LANE
    nb = 2 if b % 2 == 0 else 1
    fwd = pl.BlockSpec((nb, ts, half), lambda bi, i: (bi, i, 0))
    bwd = pl.BlockSpec((nb, ts, half), lambda bi, i: (bi, nt - 1 - i, 1))
    glf = pl.BlockSpec((nb, ncb, 2 * DN_HEADS, LANE), lambda bi, i: (bi, i, 0, 0))
    glb = pl.BlockSpec((nb, ncb, 2 * DN_HEADS, LANE), lambda bi, i: (bi, nt - 1 - i, 0, 0))
    out = jax.ShapeDtypeStruct((b, lp, half), BF16)
    return pl.pallas_call(
        functools.partial(_delta_inter_kernel, ts=ts, nb=nb),
        grid=(b // nb, nt),
        in_specs=[fwd, bwd] * 5 + [glf, glb],
        out_specs=[pl.BlockSpec((nb, ts, half), lambda bi, i: (bi, i, 0)),
                   pl.BlockSpec((nb, ts, half), lambda bi, i: (bi, nt - 1 - i, 0))],
        out_shape=[out, out],
        scratch_shapes=[pltpu.VMEM((nb * DN_HEADS, 2 * LANE, LANE), F32)],
        compiler_params=_params(("parallel", "arbitrary")),
        name="delta_inter",
    )(w, w, u, u, qd, qd, kd, kd, at, at, gl, gl)


def _mlp_tail(h1, g_ref, w1_ref, w2_ref, o_ref):
    hn = _rms_rows(h1, g_ref[...]).astype(BF16)
    acc = jnp.zeros_like(h1)
    for c in range(D_FF // FF_CHUNK):
        sl = slice(c * FF_CHUNK, (c + 1) * FF_CHUNK)
        hid = jnp.maximum(_dot(hn, w1_ref[:, sl]), 0.0)
        acc = acc + _dot((hid * hid).astype(BF16), w2_ref[sl, :])
    o_ref[...] = h1 + acc


def _out_mlp_even_kernel(yp_ref, of_ref, ob_ref, z_ref, ng_ref, wop_ref, wod_ref, h_ref, g_ref, w1_ref, w2_ref,
                         o_ref):
    ones_bf = jnp.ones((LANE, LANE), BF16)
    h1 = h_ref[...] + _dot(yp_ref[...], wop_ref[...])
    sls = [slice(hh * LANE, (hh + 1) * LANE) for hh in range(DN_HEADS)]
    os_ = [of_ref[:, sl].astype(F32) + ob_ref[:, sl].astype(F32) for sl in sls]
    mss = [_lane_sum(o * o, ones_bf) * (1.0 / LANE) for o in os_]
    ys = []
    for sl, o, ms in zip(sls, os_, mss):
        zf = z_ref[:, sl].astype(F32)
        ys.append((o * lax.rsqrt(ms + EPS) * ng_ref[...] * (zf * _sigmoid(zf))).astype(BF16))
    h1 = h1 + _dot(jnp.concatenate(ys, axis=1), wod_ref[...])
    _mlp_tail(h1, g_ref, w1_ref, w2_ref, o_ref)


def _out_mlp_even(yp, of, ob, z, ng, wop, wod, h2, g, w1, w2, tt):
    r = h2.shape[0]
    row = lambda n: pl.BlockSpec((tt, n), lambda i: (i, 0))
    return pl.pallas_call(
        _out_mlp_even_kernel,
        grid=(r // tt,),
        in_specs=[row(POOL_WIDTH), row(DN_V), row(DN_V), row(DN_V), _const_spec((1, LANE)), _const_spec(wop.shape),
                  _const_spec(wod.shape), row(D_MODEL), _const_spec((1, D_MODEL)), _const_spec(w1.shape),
                  _const_spec(w2.shape)],
        out_specs=row(D_MODEL),
        out_shape=jax.ShapeDtypeStruct((r, D_MODEL), F32),
        compiler_params=_params(("parallel",)),
        name="out_proj_mlp_even",
    )(yp, of, ob, z, ng, wop, wod, h2, g, w1, w2)


def _out_mlp_last_kernel(*refs, n_blk):
    a_refs, h_refs = refs[:n_blk], refs[n_blk:2 * n_blk]
    wo_ref, g_ref, w1_ref, w2_ref, o_ref = refs[2 * n_blk:]
    a = jnp.concatenate([r[0] for r in a_refs], axis=0)
    h = jnp.concatenate([r[0] for r in h_refs], axis=0)
    h1 = h + _dot(a, wo_ref[...])
    hn = _rms_rows(h1, g_ref[...]).astype(BF16)
    acc = jnp.zeros_like(h1)
    for c in range(D_FF // FF_CHUNK):
        sl = slice(c * FF_CHUNK, (c + 1) * FF_CHUNK)
        hid = jnp.maximum(_dot(hn, w1_ref[:, sl]), 0.0)
        acc = acc + _dot((hid * hid).astype(BF16), w2_ref[sl, :])
    o_ref[0] = h1 + acc


def _out_mlp_last(a3, wo, h3, g, w1, w2, tt, b0, nb):
    _, lp, _ = h3.shape
    s = lp - LANE
    n_blk = tt // LANE
    last = lp // LANE - 1

    def blk(k, n):
        return pl.BlockSpec((1, LANE, n), lambda bi, j: (b0 + bi, jnp.minimum(1 + n_blk * j + k, last), 0))

    return pl.pallas_call(
        functools.partial(_out_mlp_last_kernel, n_blk=n_blk),
        grid=(nb, -(-s // tt)),
        in_specs=[blk(k, a3.shape[2]) for k in range(n_blk)] + [blk(k, D_MODEL) for k in range(n_blk)]
        + [_const_spec(wo.shape), _const_spec((1, D_MODEL)), _const_spec(w1.shape), _const_spec(w2.shape)],
        out_specs=pl.BlockSpec((1, tt, D_MODEL), lambda bi, j: (bi, j, 0)),
        out_shape=jax.ShapeDtypeStruct((nb, s, D_MODEL), F32),
        compiler_params=_params(("parallel", "parallel")),
        name="out_proj_mlp_last",
    )(*([a3] * n_blk), *([h3] * n_blk), wo, g, w1, w2)


def _in_odd_kernel(h_ref, g_ref, wq_ref, wk_ref, wv_ref, qn_ref, kn_ref, cos_ref, sin_ref, q_ref, k_ref, v_ref):
    xn = _rms_rows(h_ref[...], g_ref[...]).astype(BF16)
    ones_bf = jnp.ones((LANE, LANE), BF16)
    cos = cos_ref[...]
    sin = sin_ref[...]

    def norm_rope(x, gain, scale):
        ms = _lane_sum(x * x, ones_bf) * (1.0 / ATT_HD)
        y = x * lax.rsqrt(ms + EPS) * gain
        return (y * cos + pltpu.roll(y, ATT_HD // 2, axis=1) * sin) * scale

    uq = _dot(xn, wq_ref[...])
    for hh in range(ATT_HEADS):
        sl = slice(hh * ATT_HD, (hh + 1) * ATT_HD)
        q_ref[:, sl] = norm_rope(uq[:, sl], qn_ref[...], ATT_HD ** -0.5 * LOG2E).astype(BF16)
    uk = _dot(xn, wk_ref[...])
    for hh in range(ATT_KV_HEADS):
        sl = slice(hh * ATT_HD, (hh + 1) * ATT_HD)
        k_ref[:, sl] = norm_rope(uk[:, sl], kn_ref[...], 1.0).astype(BF16)
    v_ref[...] = _dot(xn, wv_ref[...]).astype(BF16)


def _in_odd(h2, g, wq, wk, wv, qn, kn, cos, sin, tt):
    r = h2.shape[0]
    ntb = cos.shape[0] // tt
    row = lambda n: pl.BlockSpec((tt, n), lambda i: (i, 0))
    tab = pl.BlockSpec((tt, ATT_HD), lambda i: (i % ntb, 0))
    nq, nk = ATT_HEADS * ATT_HD, ATT_KV_HEADS * ATT_HD
    return pl.pallas_call(
        _in_odd_kernel,
        grid=(r // tt,),
        in_specs=[row(D_MODEL), _const_spec((1, D_MODEL)), _const_spec(wq.shape), _const_spec(wk.shape),
                  _const_spec(wv.shape), _const_spec((1, ATT_HD)), _const_spec((1, ATT_HD)), tab, tab],
        out_specs=[row(nq), row(nk), row(nk)],
        out_shape=[jax.ShapeDtypeStruct((r, nq), BF16), jax.ShapeDtypeStruct((r, nk), BF16),
                   jax.ShapeDtypeStruct((r, nk), BF16)],
        compiler_params=_params(("parallel",)),
        name="in_proj_odd",
    )(h2, g, wq, wk, wv, qn, kn, cos, sin)


def _attn_kernel(q_ref, k_ref, v_ref, o_ref, kp_ref, vaug_ref, kmax_ref, qs_ref, b_ref, acc_ref, *, tqb, tk, lp):
    sub = LANE
    n_sub = tqb // sub
    lpk = kp_ref.shape[0]
    n_kc = lpk // tk
    rows_q = ATT_GROUP * sub
    ones_bf = jnp.ones((LANE, LANE), BF16)

    @pl.when(pl.program_id(2) == 0)
    def _():
        def fill(c, kmax):
            rows = pl.ds(pl.multiple_of(c * tqb, tqb), tqb)
            valid = (lax.broadcasted_iota(jnp.int32, (tqb, LANE), 0) + c * tqb) >= FRONT
            vaug_ref[rows, :ATT_HD] = jnp.where(valid, v_ref[0, rows, :], jnp.zeros((tqb, ATT_HD), BF16))
            vaug_ref[rows, ATT_HD:] = jnp.where(valid, 1.0, 0.0).astype(BF16)
            kc = k_ref[0, rows, :]
            kp_ref[rows, :] = kc
            kf = kc.astype(F32)
            n2 = jnp.where(valid, _lane_sum(kf * kf, ones_bf), 0.0)
            return jnp.maximum(kmax, jnp.max(n2, axis=0, keepdims=True))

        kmax = lax.fori_loop(0, lp // tqb, fill, jnp.zeros((1, LANE), F32))
        kmax_ref[...] = jnp.broadcast_to(kmax, kmax_ref.shape)
        if lpk > lp:
            kp_ref[lp:, :] = jnp.zeros((lpk - lp, ATT_HD), BF16)
            vaug_ref[lp:, :] = jnp.zeros((lpk - lp, 2 * ATT_HD), BF16)

    kmax2 = kmax_ref[0:1, :]
    qss = []
    for si in range(n_sub):
        q4 = q_ref[0, si * sub:(si + 1) * sub, :]
        qss.append(jnp.concatenate([q4[:, g * ATT_HD:(g + 1) * ATT_HD] for g in range(ATT_GROUP)], axis=0))
        qs_ref[si] = qss[si]
        acc_ref[si] = jnp.zeros((rows_q, 2 * ATT_HD), F32)
    qn2s = [_lane_sum(qs.astype(F32) * qs.astype(F32), ones_bf) for qs in qss]
    for si, qn2 in enumerate(qn2s):
        b_ref[si] = jnp.sqrt(qn2 * kmax2) * (1.0 + 2.0 ** -10)

    def fast(c, _):
        rows = pl.ds(pl.multiple_of(c * tk, tk), tk)
        kc = kp_ref[rows, :]
        va = vaug_ref[rows, :]
        for si in range(n_sub):
            s = _dot_nt(qs_ref[si], kc)
            p = jnp.exp2(s - pltpu.repeat(b_ref[si], tk // LANE, axis=1)).astype(BF16)
            acc_ref[si] += _dot(p, va)
        return 0

    lax.fori_loop(0, n_kc, fast, 0, unroll=True)

    def write(si, acc):
        out = (acc[:, :ATT_HD] / acc[:, ATT_HD:]).astype(BF16)
        start = si * sub if isinstance(si, int) else pl.multiple_of(si * sub, sub)
        for g in range(ATT_GROUP):
            o_ref[0, pl.ds(start, sub), g * ATT_HD:(g + 1) * ATT_HD] = out[g * sub:(g + 1) * sub]

    l_min = None
    for si in range(n_sub):
        acc = acc_ref[si]
        write(si, acc)
        m = jnp.min(acc[:, ATT_HD:])
        l_min = m if l_min is None else jnp.minimum(l_min, m)

    @pl.when(jnp.logical_not(l_min >= SOFTMAX_MIN_DENOM))
    def _():
        col = lax.broadcasted_iota(jnp.int32, (rows_q, tk), 1)

        def redo(si, _):
            qs = qs_ref[si]

            def step(c, carry):
                m, acc = carry
                rows = pl.ds(pl.multiple_of(c * tk, tk), tk)
                key = col + c * tk
                s = jnp.where((key >= FRONT) & (key < lp), _dot_nt(qs, kp_ref[rows, :]), -jnp.inf)
                m_new = jnp.maximum(m, jnp.max(s, axis=1, keepdims=True))
                p = jnp.exp2(s - m_new).astype(BF16)
                return m_new, jnp.exp2(m - m_new) * acc + _dot(p, vaug_ref[rows, :])

            init = (jnp.full((rows_q, 1), -jnp.inf, F32), jnp.zeros((rows_q, 2 * ATT_HD), F32))
            _, acc = lax.fori_loop(0, n_kc, step, init)
            write(si, acc)
            return 0

        lax.fori_loop(0, n_sub, redo, 0)


def _attention(q, k, v, tqb):
    b, lp, _ = q.shape
    gw = ATT_GROUP * ATT_HD
    n_sub = tqb // LANE
    rows_q = ATT_GROUP * LANE
    qspec = pl.BlockSpec((1, tqb, gw), lambda bi, hi, i: (bi, i, hi))
    kvspec = pl.BlockSpec((1, lp, ATT_HD), lambda bi, hi, i: (bi, 0, hi))
    tk = min(3 * MXU_TILE, -(-lp // MXU_TILE) * MXU_TILE)
    lpk = -(-lp // tk) * tk
    return pl.pallas_call(
        functools.partial(_attn_kernel, tqb=tqb, tk=tk, lp=lp),
        grid=(b, ATT_KV_HEADS, lp // tqb),
        in_specs=[qspec, kvspec, kvspec],
        out_specs=qspec,
        out_shape=jax.ShapeDtypeStruct((b, lp, ATT_HEADS * ATT_HD), BF16),
        scratch_shapes=[pltpu.VMEM((lpk, ATT_HD), BF16), pltpu.VMEM((lpk, 2 * ATT_HD), BF16),
                        pltpu.VMEM((8, LANE), F32),
                        pltpu.VMEM((n_sub, rows_q, ATT_HD), BF16), pltpu.VMEM((n_sub, rows_q, LANE), F32),
                        pltpu.VMEM((n_sub, rows_q, 2 * ATT_HD), F32)],
        compiler_params=_params(("parallel", "parallel", "arbitrary")),
        name="gqa_attention",
    )(q, k, v)


def _rope_tables(lp):
    s = lp - LANE
    pos = np.arange(s)
    freqs = jnp.asarray(ROPE_THETA, F32) ** (-(jnp.arange(ATT_HD // 4, dtype=F32) / (ATT_HD // 4)))
    row = jnp.asarray(pos // GRID_W, F32)
    colp = jnp.asarray(pos % GRID_W, F32)
    ang = jnp.concatenate([row[:, None] * freqs, colp[:, None] * freqs], axis=-1)
    ang = jnp.concatenate([jnp.zeros((LANE, ATT_HD // 2), F32), ang], axis=0)
    c, sn = jnp.cos(ang), jnp.sin(ang)
    return jnp.concatenate([c, c], axis=-1), jnp.concatenate([-sn, sn], axis=-1)


def _trunk(xa, xb, meta_tokens, mix_norm, mlp_norm, w_in_even, pool_w, pool_scale, conv_qkv, a_log, dt_bias,
           delta_norm, w_out_even, w_in_odd, q_norm, k_norm, w_out_odd, w_mlp_in, w_mlp_out):
    groups = (xa.shape[0], xb.shape[0])
    b = sum(groups)
    s = xa.shape[1]
    lp = s + LANE
    tt = _token_tile(lp)
    r = b * lp

    o1 = POOL_WIDTH
    o2 = o1 + 2 * DN_QK + DN_V
    o3 = o2 + DN_V
    w_in = w_in_even[0]
    wb = jnp.zeros((D_MODEL, LANE), F32).at[:, :4 * DN_HEADS].set(w_in[:, o3:])
    meta_pad = jnp.concatenate([jnp.zeros((FRONT, D_MODEL), F32), meta_tokens], axis=0)
    h2, up, uq, z, ba = _in_even(xa, xb, meta_pad, mix_norm[0][None], w_in[:, :o1].astype(BF16),
                                 w_in[:, o1:o2].astype(BF16), w_in[:, o2:o3].astype(BF16), wb.astype(BF16), tt)
    gate_p = jnp.zeros((8, LANE), F32)
    gate_p = gate_p.at[0, 2 * DN_HEADS:4 * DN_HEADS].set(-jnp.exp(a_log[0].reshape(-1)))
    gate_p = gate_p.at[1, 2 * DN_HEADS:4 * DN_HEADS].set(dt_bias[0].reshape(-1))
    conv_w = jnp.zeros((8, 2 * DN_QK + DN_V), F32).at[:DN_CONV].set(conv_qkv[0])
    seq = lambda a: a.reshape(b, lp, a.shape[-1])
    y_pool, q, k, v, gt = _prep(seq(up), seq(uq), seq(ba), pool_w[0].astype(BF16), pool_scale[0][None], conv_w,
                                gate_p, tt)
    w, u, qd, kd, at, gl = _delta_intra(q, k, v, gt, tt)
    o_f, o_b = _delta_inter(w, u, qd, kd, at, gl, tt)
    wo = w_out_even[0].astype(BF16)
    flat = lambda a: a.reshape(r, a.shape[-1])
    h2 = _out_mlp_even(flat(y_pool), flat(o_f), flat(o_b), z, delta_norm[0][None], wo[:POOL_WIDTH], wo[POOL_WIDTH:],
                       h2, mlp_norm[0][None], w_mlp_in[0].astype(BF16), w_mlp_out[0].astype(BF16), tt)

    nq, nk = ATT_HEADS * ATT_HD, ATT_KV_HEADS * ATT_HD
    w_in = w_in_odd[0].astype(BF16)
    cos, sin = _rope_tables(lp)
    qa, ka, va = _in_odd(h2, mix_norm[1][None], w_in[:, :nq], w_in[:, nq:nq + nk], w_in[:, nq + nk:],
                         q_norm[0][None], k_norm[0][None], cos, sin, tt)
    att = _attention(seq(qa), seq(ka), seq(va), tt)
    outs, b0 = [], 0
    for nb in groups:
        outs.append(_out_mlp_last(att, w_out_odd[0].astype(BF16), h2.reshape(b, lp, D_MODEL), mlp_norm[1][None],
                                  w_mlp_in[1].astype(BF16), w_mlp_out[1].astype(BF16), tt, b0, nb))
        b0 += nb
    return tuple(outs)


def kernel(x_prompt, x_sample, meta_tokens, mix_norm, mlp_norm, w_in_even, pool_w, pool_scale, conv_qkv, a_log,
           dt_bias, delta_norm, w_out_even, w_in_odd, q_norm, k_norm, w_out_odd, w_mlp_in, w_mlp_out):
    return _trunk(x_prompt, x_sample, meta_tokens, mix_norm, mlp_norm, w_in_even, pool_w, pool_scale, conv_qkv, a_log,
                  dt_bias, delta_norm, w_out_even, w_in_odd, q_norm, k_norm, w_out_odd, w_mlp_in, w_mlp_out)
```

```python
import functools
import math

import jax
import jax.numpy as jnp
import numpy as np
from jax import lax
from jax.experimental import pallas as pl
from jax.experimental.pallas import tpu as pltpu

F32 = jnp.float32
BF16 = jnp.bfloat16

D_MODEL = 1024
N_META = 16
LANE = 128
MXU_TILE = 256
HALO = 16
FRONT = LANE - N_META
GRID_W = 64
EPS = 1e-6
POOL_WINDOWS = (2, 4, 8, 16)
POOL_WIDTH = 512
DN_HEADS = 4
DN_CONV = 7
CHUNK = 64
PAIR = 2 * CHUNK
ROWB = 128
DN_QK = 512
DN_V = 512
ATT_HD = 128
ATT_HEADS = 8
ATT_KV_HEADS = 2
ATT_GROUP = 4
ROPE_THETA = 10000.0
D_FF = 4096
FF_CHUNK = 1024
VMEM_LIMIT = 56 * 1024 * 1024
LOG2E = math.log2(math.e)
SOFTMAX_MIN_DENOM = 1e-26


def _token_tile(lp):
    for t in (640, 512, 384, 256, 128):
        if lp % t == 0:
            return t
    raise ValueError(lp)


def _params(sem):
    return pltpu.CompilerParams(dimension_semantics=sem, vmem_limit_bytes=VMEM_LIMIT)


def _const_spec(shape):
    nd = len(shape)
    return pl.BlockSpec(shape, lambda *_: (0,) * nd, pipeline_mode=pl.Buffered(1))


def _dot(a, b):
    return jnp.dot(a, b, preferred_element_type=F32)


def _dot_nt(a, b):
    return lax.dot_general(a, b, (((1,), (1,)), ((), ())), preferred_element_type=F32)


def _dot_tn(a, b):
    return lax.dot_general(a, b, (((0,), (0,)), ((), ())), preferred_element_type=F32)


def _split2(x):
    hi = x.astype(BF16)
    lo = (x - hi.astype(F32)).astype(BF16)
    return hi, lo


def _split3(x):
    hi = x.astype(BF16)
    r = x - hi.astype(F32)
    mid = r.astype(BF16)
    lo = (r - mid.astype(F32)).astype(BF16)
    return hi, mid, lo


def _lane_sum(x, ones_bf):
    hi, lo = _split2(x)
    return _dot(hi, ones_bf) + _dot(lo, ones_bf)


def _sigmoid(x):
    return 1.0 / (1.0 + jnp.exp(-x))


def _rms_rows(x, g):
    ms = jnp.mean(x * x, axis=-1, keepdims=True)
    return x * lax.rsqrt(ms + EPS) * g


def _in_even_kernel(*refs, n_blk, nt, nb0):
    xa_refs, xb_refs = refs[:n_blk], refs[n_blk:2 * n_blk]
    mp_ref, g_ref, wp_ref, wq_ref, wz_ref, wb_ref, h_ref, up_ref, uq_ref, z_ref, ba_ref = refs[2 * n_blk:]
    i = pl.program_id(0)
    first_group = (i // nt) < nb0
    pieces = [jnp.where(first_group, xa[0], xb[0]) for xa, xb in zip(xa_refs, xb_refs)]
    pieces[0] = jnp.where((i % nt) == 0, mp_ref[...], pieces[0])
    h = jnp.concatenate(pieces, axis=0)
    h_ref[...] = h
    xn = _rms_rows(h, g_ref[...]).astype(BF16)
    up_ref[...] = _dot(xn, wp_ref[...]).astype(BF16)
    uq_ref[...] = _dot(xn, wq_ref[...]).astype(BF16)
    z_ref[...] = _dot(xn, wz_ref[...]).astype(BF16)
    ba_ref[...] = _dot(xn, wb_ref[...])


def _in_even(xa, xb, meta_pad, g, wp, wq, wz, wb, tt):
    nb0, s, _ = xa.shape
    nb1 = xb.shape[0]
    lp = s + LANE
    nt = lp // tt
    n_blk = tt // LANE
    r = (nb0 + nb1) * lp

    def xspec(k, g0, nbg):
        def index(i):
            bi, j = i // nt, i % nt
            mine = (bi >= g0) & (bi < g0 + nbg)
            return (jnp.clip(bi - g0, 0, nbg - 1), jnp.where(mine, jnp.maximum(n_blk * j + k - 1, 0), 0), 0)
        return pl.BlockSpec((1, LANE, D_MODEL), index)

    row = lambda n: pl.BlockSpec((tt, n), lambda i: (i, 0))
    return pl.pallas_call(
        functools.partial(_in_even_kernel, n_blk=n_blk, nt=nt, nb0=nb0),
        grid=(r // tt,),
        in_specs=[xspec(k, 0, nb0) for k in range(n_blk)] + [xspec(k, nb0, nb1) for k in range(n_blk)]
        + [_const_spec((LANE, D_MODEL)), _const_spec((1, D_MODEL)), _const_spec(wp.shape), _const_spec(wq.shape),
           _const_spec(wz.shape), _const_spec(wb.shape)],
        out_specs=[row(D_MODEL), row(POOL_WIDTH), row(2 * DN_QK + DN_V), row(DN_V), row(LANE)],
        out_shape=[jax.ShapeDtypeStruct((r, D_MODEL), F32), jax.ShapeDtypeStruct((r, POOL_WIDTH), BF16),
                   jax.ShapeDtypeStruct((r, 2 * DN_QK + DN_V), BF16), jax.ShapeDtypeStruct((r, DN_V), BF16),
                   jax.ShapeDtypeStruct((r, LANE), F32)],
        compiler_params=_params(("parallel",)),
        name="in_proj_even",
    )(*([xa] * n_blk), *([xb] * n_blk), meta_pad, g, wp, wq, wz, wb)


def _prep_kernel(up_ref, upl_ref, upr_ref, uq_ref, uql_ref, uqr_ref, ba_ref, band_ref, tri_ref, pw_ref, ps_ref,
                 cw_ref, gp_ref, yp_ref, q_ref, k_ref, v_ref, gt_ref, extp_ref, extq_ref, vb_ref, *, ts, lp):
    i = pl.program_id(1)
    row0 = i * ts
    next_ = ROWB + 2 * HALO
    rows_ext = lax.broadcasted_iota(jnp.int32, (ts + 2 * HALO, LANE), 0) + (row0 - HALO)
    valid_ext = (rows_ext >= FRONT) & (rows_ext < lp)
    ones_bf = jnp.ones((LANE, LANE), BF16)
    lane = lax.broadcasted_iota(jnp.int32, (ROWB, LANE), 1)
    n_blk = (2 * DN_QK + DN_V) // LANE

    vb_ref[...] = jnp.where(valid_ext, 1.0, 0.0).astype(BF16)
    for gi in range(len(POOL_WINDOWS)):
        sl = slice(gi * LANE, (gi + 1) * LANE)
        eg = jnp.concatenate([upl_ref[0, :, sl], up_ref[0, :, sl], upr_ref[0, :, sl]], axis=0)
        extp_ref[:, sl] = jnp.where(valid_ext, eg, jnp.zeros_like(eg))
    for cb in range(n_blk):
        sl = slice(cb * LANE, (cb + 1) * LANE)
        e = jnp.concatenate([uql_ref[0, :, sl], uq_ref[0, :, sl], uqr_ref[0, :, sl]], axis=0).astype(F32)
        extq_ref[cb] = jnp.where(valid_ext, e, 0.0)

    for rb in range(ts // ROWB):
        r0 = rb * ROWB
        rows = slice(r0, r0 + ROWB)
        valid_main = valid_ext[HALO + r0:HALO + r0 + ROWB]

        vb = vb_ref[r0:r0 + next_, :]
        scs = [_dot(band_ref[gi], jnp.concatenate([extp_ref[r0:r0 + next_, gi * LANE:(gi + 1) * LANE], vb], axis=1))
               for gi in range(len(POOL_WINDOWS))]
        ds_ = [(sc[:, :LANE] / jnp.maximum(sc[:, LANE:], 1.0)
                - up_ref[0, rows, gi * LANE:(gi + 1) * LANE].astype(F32)).astype(BF16) for gi, sc in enumerate(scs)]
        for gi, d in enumerate(ds_):
            sl = slice(gi * LANE, (gi + 1) * LANE)
            y = _dot(d, pw_ref[gi]) * ps_ref[:, sl]
            yp_ref[0, rows, sl] = jnp.where(valid_main, y, 0.0).astype(BF16)

        xs = []
        for cb in range(n_blk):
            sl = slice(cb * LANE, (cb + 1) * LANE)
            acc = jnp.zeros((ROWB, LANE), F32)
            for j in range(DN_CONV):
                acc = acc + cw_ref[j:j + 1, sl] * extq_ref[cb, pl.ds(r0 + HALO - DN_CONV // 2 + j, ROWB), :]
            xs.append(jnp.where(valid_main, acc * _sigmoid(acc), 0.0))
        sss = [_lane_sum(x * x, ones_bf) for x in xs[:2 * DN_HEADS]]
        for cb, x in enumerate(xs):
            hsl = slice((cb % DN_HEADS) * LANE, (cb % DN_HEADS + 1) * LANE)
            if cb < DN_HEADS:
                q_ref[0, rows, hsl] = (x * lax.rsqrt(sss[cb] + EPS) * (LANE ** -0.5)).astype(BF16)
            elif cb < 2 * DN_HEADS:
                k_ref[0, rows, hsl] = (x * lax.rsqrt(sss[cb] + EPS)).astype(BF16)
            else:
                v_ref[0, rows, hsl] = x.astype(BF16)

        ba = ba_ref[0, rows, :]
        beta = jnp.where(valid_main, _sigmoid(ba), 0.0)
        xg = ba + gp_ref[1:2, :]
        softplus = jnp.maximum(xg, 0.0) + jnp.log1p(jnp.exp(-jnp.abs(xg)))
        g = jnp.where(valid_main, gp_ref[0:1, :] * softplus, 0.0)
        g3 = _split3(g)
        gcf = _dot(tri_ref[0], g3[0]) + _dot(tri_ref[0], g3[1]) + _dot(tri_ref[0], g3[2])
        gcb = _dot(tri_ref[1], g3[0]) + _dot(tri_ref[1], g3[1]) + _dot(tri_ref[1], g3[2])
        gt_ref[0, rows, :] = jnp.where(lane < 2 * DN_HEADS, beta,
                                       jnp.where(lane < 3 * DN_HEADS, gcf, jnp.where(lane < 4 * DN_HEADS, gcb, 0.0)))


def _prep_constants():
    t = np.arange(ROWB)[:, None]
    j = np.arange(ROWB + 2 * HALO)[None, :] - HALO
    bands = []
    for w in POOL_WINDOWS:
        bands.append(((j >= t - w // 2) & (j <= t + (w - 1 - w // 2))).astype(np.float32))
    band = jnp.asarray(np.stack(bands), BF16)
    a = np.arange(ROWB)
    same = (a[:, None] // CHUNK) == (a[None, :] // CHUNK)
    tri = np.stack([same & (a[None, :] <= a[:, None]), same & (a[None, :] >= a[:, None])]).astype(np.float32)
    return band, jnp.asarray(tri, BF16)


def _prep(up, uq, ba, pool_w, pool_scale, conv_w, gate_p, ts):
    b, lp, _ = up.shape
    nt = lp // ts
    hb = ts // HALO
    band, tri = _prep_constants()
    cqkv = 2 * DN_QK + DN_V
    main = lambda n: pl.BlockSpec((1, ts, n), lambda bi, i: (bi, i, 0))
    left = lambda n: pl.BlockSpec((1, HALO, n), lambda bi, i: (bi, jnp.maximum(i * hb - 1, 0), 0))
    right = lambda n: pl.BlockSpec((1, HALO, n), lambda bi, i: (bi, jnp.minimum((i + 1) * hb, lp // HALO - 1), 0))
    out = lambda n, dt: jax.ShapeDtypeStruct((b, lp, n), dt)
    return pl.pallas_call(
        functools.partial(_prep_kernel, ts=ts, lp=lp),
        grid=(b, nt),
        in_specs=[main(POOL_WIDTH), left(POOL_WIDTH), right(POOL_WIDTH), main(cqkv), left(cqkv), right(cqkv),
                  main(LANE), _const_spec(band.shape), _const_spec(tri.shape), _const_spec(pool_w.shape),
                  _const_spec(pool_scale.shape), _const_spec(conv_w.shape), _const_spec(gate_p.shape)],
        out_specs=[main(POOL_WIDTH), main(DN_QK), main(DN_QK), main(DN_V), main(LANE)],
        out_shape=[out(POOL_WIDTH, BF16), out(DN_QK, BF16), out(DN_QK, BF16), out(DN_V, BF16), out(LANE, F32)],
        scratch_shapes=[pltpu.VMEM((ts + 2 * HALO, POOL_WIDTH), BF16), pltpu.VMEM((cqkv // LANE, ts + 2 * HALO, LANE), F32),
                        pltpu.VMEM((ts + 2 * HALO, LANE), BF16)],
        compiler_params=_params(("parallel", "parallel")),
        name="mixer_prep",
    )(up, up, up, uq, uq, uq, ba, band, tri, pool_w, pool_scale, conv_w, gate_p)


def _pair_masks():
    r = lax.broadcasted_iota(jnp.int32, (PAIR, PAIR), 0)
    c = lax.broadcasted_iota(jnp.int32, (PAIR, PAIR), 1)
    top = (r < CHUNK) & (c < CHUNK)
    bot = (r >= CHUNK) & (c >= CHUNK)
    incl = (top & (r >= c)) | (bot & (r <= c))
    strict = (top & (r > c)) | (bot & (r < c))
    m16 = (r // 16) == (c // 16)
    m32 = (r // 32) == (c // 32)
    eye = (r == c).astype(F32)
    return incl, strict, m16, (m32 & ~m16, ~m32), eye


def _unit_tri_inverse(lmats, m16, off_masks, eye):
    ns = [-jnp.where(m16, lm, 0.0) for lm in lmats]
    ts_ = [eye + n for n in ns]
    ps = ns
    for _ in range(3):
        pbs = [p.astype(BF16) for p in ps]
        ps = [_dot(pb, pb) for pb in pbs]
        ts_ = [t + _dot(t.astype(BF16), p.astype(BF16)) for t, p in zip(ts_, ps)]
    for lo_mask in off_masks:
        tbs = [t.astype(BF16) for t in ts_]
        offs = [jnp.where(lo_mask, lm, 0.0).astype(BF16) for lm in lmats]
        mids = [_dot(tb, off).astype(BF16) for tb, off in zip(tbs, offs)]
        ts_ = [t - _dot(mid, tb) for t, mid, tb in zip(ts_, mids, tbs)]
    return ts_


def _intra_chunks(q_ref, k_ref, v_ref, gt_ref, w_ref, u_ref, qd_ref, kd_ref, at_ref, gl_ref, chunks, masks):
    incl, strict, m16, off_masks, eye = masks
    lane_row = lax.broadcasted_iota(jnp.int32, (1, PAIR), 1)
    sub_col = lax.broadcasted_iota(jnp.int32, (PAIR, 1), 0)
    nh = DN_HEADS
    inst = []
    for j in chunks:
        rows = pl.ds(pl.multiple_of(j * CHUNK, CHUNK), CHUNK)
        gt = gt_ref[0, rows, :]
        g2t = jnp.concatenate([gt, gt], axis=0).T
        for h in range(nh):
            sl = slice(h * LANE, (h + 1) * LANE)
            qc, kc, vc = q_ref[0, rows, sl], k_ref[0, rows, sl], v_ref[0, rows, sl]
            k2, q2 = jnp.concatenate([kc, kc], axis=0), jnp.concatenate([qc, qc], axis=0)
            k2f = k2.astype(F32)
            bcol = jnp.concatenate([gt[:, h:h + 1], gt[:, nh + h:nh + h + 1]], axis=0)
            gcol = jnp.concatenate([gt[:, 2 * nh + h:2 * nh + h + 1], gt[:, 3 * nh + h:3 * nh + h + 1]], axis=0)
            grow = jnp.where(lane_row < CHUNK, g2t[2 * nh + h:2 * nh + h + 1, :],
                             g2t[3 * nh + h:3 * nh + h + 1, :])
            kb2 = k2f * bcol
            inst.append(dict(j=j, h=h, rows=rows, k2=k2, q2=q2, k2f=k2f, vc=vc, bcol=bcol, gcol=gcol, grow=grow,
                             kb2=kb2))
    raws = [_dot_nt(jnp.concatenate([it["kb2"].astype(BF16), it["q2"]], axis=0), it["k2"]) for it in inst]
    decays = [jnp.exp(jnp.where(incl, it["gcol"] - it["grow"], -jnp.inf)) for it in inst]
    lmats = [jnp.where(strict, raw[:PAIR] * dec, 0.0) for raw, dec in zip(raws, decays)]
    tinv = _unit_tri_inverse(lmats, m16, off_masks, eye)
    rhss = []
    for it in inst:
        e2 = jnp.exp(it["gcol"])
        v2f = jnp.concatenate([it["vc"], it["vc"]], axis=0).astype(F32)
        rhss.append(jnp.concatenate([v2f * it["bcol"], it["kb2"] * e2], axis=1).astype(BF16))
        it["e2"] = e2
    uws = [_dot(t.astype(BF16), rhs) for t, rhs in zip(tinv, rhss)]
    for it, raw, dec, uw in zip(inst, raws, decays, uws):
        h, rows, gcol = it["h"], it["rows"], it["gcol"]
        attn = raw[PAIR:] * dec
        qd2 = it["q2"].astype(F32) * it["e2"]
        gl_col = jnp.where(sub_col < CHUNK, gcol[CHUNK - 1:CHUNK, :], gcol[CHUNK:CHUNK + 1, :])
        kd2 = it["k2f"] * jnp.exp(gl_col - gcol)
        for d in range(2):
            half = slice(d * CHUNK, (d + 1) * CHUNK)
            dst = slice((d * nh + h) * LANE, (d * nh + h + 1) * LANE)
            u_ref[0, rows, dst] = uw[half, :LANE].astype(BF16)
            w_ref[0, rows, dst] = uw[half, LANE:].astype(BF16)
            qd_ref[0, rows, dst] = qd2[half].astype(BF16)
            kd_ref[0, rows, dst] = kd2[half].astype(BF16)
            at_ref[0, rows, dst] = attn[half].astype(BF16)
            g_last = gcol[CHUNK - 1:CHUNK, :] if d == 0 else gcol[CHUNK:CHUNK + 1, :]
            gl_ref[0, it["j"], d * nh + h:d * nh + h + 1, :] = jnp.broadcast_to(jnp.exp(g_last), (1, LANE))


def _delta_intra_kernel(q_ref, k_ref, v_ref, gt_ref, w_ref, u_ref, qd_ref, kd_ref, at_ref, gl_ref, *, ts):
    masks = _pair_masks()

    ncb = ts // CHUNK
    per_body = next(n for n in (5, 4, 3, 2, 1) if ncb % n == 0)

    def body(jj, _):
        _intra_chunks(q_ref, k_ref, v_ref, gt_ref, w_ref, u_ref, qd_ref, kd_ref, at_ref, gl_ref,
                      tuple(per_body * jj + i for i in range(per_body)), masks)
        return 0

    lax.fori_loop(0, ncb // per_body, body, 0)


def _delta_intra(q, k, v, gt, ts):
    b, lp, _ = q.shape
    ncb = ts // CHUNK
    wide = 2 * DN_HEADS * LANE
    tile = lambda n: pl.BlockSpec((1, ts, n), lambda bi, i: (bi, i, 0))
    big = jax.ShapeDtypeStruct((b, lp, wide), BF16)
    return pl.pallas_call(
        functools.partial(_delta_intra_kernel, ts=ts),
        grid=(b, lp // ts),
        in_specs=[tile(DN_QK), tile(DN_QK), tile(DN_V), tile(LANE)],
        out_specs=[tile(wide)] * 5 + [pl.BlockSpec((1, ncb, 2 * DN_HEADS, LANE), lambda bi, i: (bi, i, 0, 0))],
        out_shape=[big] * 5 + [jax.ShapeDtypeStruct((b, lp // CHUNK, 2 * DN_HEADS, LANE), F32)],
        compiler_params=_params(("parallel", "parallel")),
        name="delta_intra",
    )(q, k, v, gt)


def _delta_inter_kernel(wf_ref, wb_ref, uf_ref, ub_ref, qdf_ref, qdb_ref, kdf_ref, kdb_ref, atf_ref, atb_ref,
                        glf_ref, glb_ref, of_ref, ob_ref, s_ref, *, ts, nb):
    ncb = ts // CHUNK
    nh = DN_HEADS

    @pl.when(pl.program_id(1) == 0)
    def _():
        s_ref[...] = jnp.zeros(s_ref.shape, F32)

    zero = jnp.zeros((CHUNK, LANE), BF16)
    left = lambda x: jnp.concatenate([x, zero], axis=1)
    right = lambda x: jnp.concatenate([zero, x], axis=1)

    chains = [(bb, h, slice(h * LANE, (h + 1) * LANE)) for bb in range(nb) for h in range(nh)]

    def body(j, _):
        jb = ncb - 1 - j
        rf = pl.ds(pl.multiple_of(j * CHUNK, CHUNK), CHUNK)
        rb = pl.ds(pl.multiple_of(jb * CHUNK, CHUNK), CHUNK)
        s2s = [s_ref[bb * nh + h] for bb, h, _ in chains]
        lhss = [jnp.concatenate([left(wf_ref[bb, rf, sl]), right(wb_ref[bb, rb, sl]),
                                 left(qdf_ref[bb, rf, sl]), right(qdb_ref[bb, rb, sl])], axis=0)
                for bb, _, sl in chains]
        r1s = [_dot(lhs, s2.astype(BF16)) for lhs, s2 in zip(lhss, s2s)]
        v_news = [(jnp.concatenate([uf_ref[bb, rf, sl], ub_ref[bb, rb, sl]], axis=0).astype(F32) - r1[:PAIR])
                  .astype(BF16) for (bb, _, sl), r1 in zip(chains, r1s)]
        o2s = [r1[PAIR:] + _dot(jnp.concatenate([atf_ref[bb, rf, sl], atb_ref[bb, rb, sl]], axis=0), v_new)
               for (bb, _, sl), r1, v_new in zip(chains, r1s, v_news)]
        upds = [_dot_tn(jnp.concatenate([left(kdf_ref[bb, rf, sl]), right(kdb_ref[bb, rb, sl])], axis=0), v_new)
                for (bb, _, sl), v_new in zip(chains, v_news)]
        for (bb, h, sl), s2, o2, upd in zip(chains, s2s, o2s, upds):
            of_ref[bb, rf, sl] = o2[:CHUNK].astype(BF16)
            ob_ref[bb, rb, sl] = o2[CHUNK:].astype(BF16)
            scale = jnp.concatenate([jnp.broadcast_to(glf_ref[bb, j, h:h + 1, :], (LANE, LANE)),
                                     jnp.broadcast_to(glb_ref[bb, jb, nh + h:nh + h + 1, :], (LANE, LANE))], axis=0)
            s_ref[bb * nh + h] = s2 * scale + upd
        return 0

    lax.fori_loop(0, ncb, body, 0)


def _delta_inter(w, u, qd, kd, at, gl, ts):
    b, lp, _ = w.shape
    nt = lp // ts
    ncb = ts // CHUNK
    half = DN_HEADS * LANE
    nb = 2 if b % 2 == 0 else 1
    fwd = pl.BlockSpec((nb, ts, half), lambda bi, i: (bi, i, 0))
    bwd = pl.BlockSpec((nb, ts, half), lambda bi, i: (bi, nt - 1 - i, 1))
    glf = pl.BlockSpec((nb, ncb, 2 * DN_HEADS, LANE), lambda bi, i: (bi, i, 0, 0))
    glb = pl.BlockSpec((nb, ncb, 2 * DN_HEADS, LANE), lambda bi, i: (bi, nt - 1 - i, 0, 0))
    out = jax.ShapeDtypeStruct((b, lp, half), BF16)
    return pl.pallas_call(
        functools.partial(_delta_inter_kernel, ts=ts, nb=nb),
        grid=(b // nb, nt),
        in_specs=[fwd, bwd] * 5 + [glf, glb],
        out_specs=[pl.BlockSpec((nb, ts, half), lambda bi, i: (bi, i, 0)),
                   pl.BlockSpec((nb, ts, half), lambda bi, i: (bi, nt - 1 - i, 0))],
        out_shape=[out, out],
        scratch_shapes=[pltpu.VMEM((nb * DN_HEADS, 2 * LANE, LANE), F32)],
        compiler_params=_params(("parallel", "arbitrary")),
        name="delta_inter",
    )(w, w, u, u, qd, qd, kd, kd, at, at, gl, gl)


def _mlp_tail(h1, g_ref, w1_ref, w2_ref, o_ref):
    hn = _rms_rows(h1, g_ref[...]).astype(BF16)
    acc = jnp.zeros_like(h1)
    for c in range(D_FF // FF_CHUNK):
        sl = slice(c * FF_CHUNK, (c + 1) * FF_CHUNK)
        hid = jnp.maximum(_dot(hn, w1_ref[:, sl]), 0.0)
        acc = acc + _dot((hid * hid).astype(BF16), w2_ref[sl, :])
    o_ref[...] = h1 + acc


def _out_mlp_even_kernel(yp_ref, of_ref, ob_ref, z_ref, ng_ref, wop_ref, wod_ref, h_ref, g_ref, w1_ref, w2_ref,
                         o_ref):
    ones_bf = jnp.ones((LANE, LANE), BF16)
    h1 = h_ref[...] + _dot(yp_ref[...], wop_ref[...])
    sls = [slice(hh * LANE, (hh + 1) * LANE) for hh in range(DN_HEADS)]
    os_ = [of_ref[:, sl].astype(F32) + ob_ref[:, sl].astype(F32) for sl in sls]
    mss = [_lane_sum(o * o, ones_bf) * (1.0 / LANE) for o in os_]
    ys = []
    for sl, o, ms in zip(sls, os_, mss):
        zf = z_ref[:, sl].astype(F32)
        ys.append((o * lax.rsqrt(ms + EPS) * ng_ref[...] * (zf * _sigmoid(zf))).astype(BF16))
    h1 = h1 + _dot(jnp.concatenate(ys, axis=1), wod_ref[...])
    _mlp_tail(h1, g_ref, w1_ref, w2_ref, o_ref)


def _out_mlp_even(yp, of, ob, z, ng, wop, wod, h2, g, w1, w2, tt):
    r = h2.shape[0]
    row = lambda n: pl.BlockSpec((tt, n), lambda i: (i, 0))
    return pl.pallas_call(
        _out_mlp_even_kernel,
        grid=(r // tt,),
        in_specs=[row(POOL_WIDTH), row(DN_V), row(DN_V), row(DN_V), _const_spec((1, LANE)), _const_spec(wop.shape),
                  _const_spec(wod.shape), row(D_MODEL), _const_spec((1, D_MODEL)), _const_spec(w1.shape),
                  _const_spec(w2.shape)],
        out_specs=row(D_MODEL),
        out_shape=jax.ShapeDtypeStruct((r, D_MODEL), F32),
        compiler_params=_params(("parallel",)),
        name="out_proj_mlp_even",
    )(yp, of, ob, z, ng, wop, wod, h2, g, w1, w2)


def _out_mlp_last_kernel(*refs, n_blk):
    a_refs, h_refs = refs[:n_blk], refs[n_blk:2 * n_blk]
    wo_ref, g_ref, w1_ref, w2_ref, o_ref = refs[2 * n_blk:]
    a = jnp.concatenate([r[0] for r in a_refs], axis=0)
    h = jnp.concatenate([r[0] for r in h_refs], axis=0)
    h1 = h + _dot(a, wo_ref[...])
    hn = _rms_rows(h1, g_ref[...]).astype(BF16)
    acc = jnp.zeros_like(h1)
    for c in range(D_FF // FF_CHUNK):
        sl = slice(c * FF_CHUNK, (c + 1) * FF_CHUNK)
        hid = jnp.maximum(_dot(hn, w1_ref[:, sl]), 0.0)
        acc = acc + _dot((hid * hid).astype(BF16), w2_ref[sl, :])
    o_ref[0] = h1 + acc


def _out_mlp_last(a3, wo, h3, g, w1, w2, tt, b0, nb):
    _, lp, _ = h3.shape
    s = lp - LANE
    n_blk = tt // LANE
    last = lp // LANE - 1

    def blk(k, n):
        return pl.BlockSpec((1, LANE, n), lambda bi, j: (b0 + bi, jnp.minimum(1 + n_blk * j + k, last), 0))

    return pl.pallas_call(
        functools.partial(_out_mlp_last_kernel, n_blk=n_blk),
        grid=(nb, -(-s // tt)),
        in_specs=[blk(k, a3.shape[2]) for k in range(n_blk)] + [blk(k, D_MODEL) for k in range(n_blk)]
        + [_const_spec(wo.shape), _const_spec((1, D_MODEL)), _const_spec(w1.shape), _const_spec(w2.shape)],
        out_specs=pl.BlockSpec((1, tt, D_MODEL), lambda bi, j: (bi, j, 0)),
        out_shape=jax.ShapeDtypeStruct((nb, s, D_MODEL), F32),
        compiler_params=_params(("parallel", "parallel")),
        name="out_proj_mlp_last",
    )(*([a3] * n_blk), *([h3] * n_blk), wo, g, w1, w2)


def _in_odd_kernel(h_ref, g_ref, wq_ref, wk_ref, wv_ref, qn_ref, kn_ref, cos_ref, sin_ref, q_ref, k_ref, v_ref):
    xn = _rms_rows(h_ref[...], g_ref[...]).astype(BF16)
    ones_bf = jnp.ones((LANE, LANE), BF16)
    cos = cos_ref[...]
    sin = sin_ref[...]

    def rope(x, ms, gain, scale):
        y = x * lax.rsqrt(ms * (1.0 / ATT_HD) + EPS) * gain
        return (y * cos + pltpu.roll(y, ATT_HD // 2, axis=1) * sin) * scale

    uq = _dot(xn, wq_ref[...])
    uk = _dot(xn, wk_ref[...])
    v_ref[...] = _dot(xn, wv_ref[...]).astype(BF16)
    heads = [uq[:, hh * ATT_HD:(hh + 1) * ATT_HD] for hh in range(ATT_HEADS)] \
        + [uk[:, hh * ATT_HD:(hh + 1) * ATT_HD] for hh in range(ATT_KV_HEADS)]
    sums = [_lane_sum(x * x, ones_bf) for x in heads]
    for hh in range(ATT_HEADS):
        q_ref[:, hh * ATT_HD:(hh + 1) * ATT_HD] = rope(heads[hh], sums[hh], qn_ref[...],
                                                       ATT_HD ** -0.5 * LOG2E).astype(BF16)
    for hh in range(ATT_KV_HEADS):
        k_ref[:, hh * ATT_HD:(hh + 1) * ATT_HD] = rope(heads[ATT_HEADS + hh], sums[ATT_HEADS + hh], kn_ref[...],
                                                       1.0).astype(BF16)


def _in_odd(h2, g, wq, wk, wv, qn, kn, cos, sin, tt):
    r = h2.shape[0]
    ntb = cos.shape[0] // tt
    row = lambda n: pl.BlockSpec((tt, n), lambda i: (i, 0))
    tab = pl.BlockSpec((tt, ATT_HD), lambda i: (i % ntb, 0))
    nq, nk = ATT_HEADS * ATT_HD, ATT_KV_HEADS * ATT_HD
    return pl.pallas_call(
        _in_odd_kernel,
        grid=(r // tt,),
        in_specs=[row(D_MODEL), _const_spec((1, D_MODEL)), _const_spec(wq.shape), _const_spec(wk.shape),
                  _const_spec(wv.shape), _const_spec((1, ATT_HD)), _const_spec((1, ATT_HD)), tab, tab],
        out_specs=[row(nq), row(nk), row(nk)],
        out_shape=[jax.ShapeDtypeStruct((r, nq), BF16), jax.ShapeDtypeStruct((r, nk), BF16),
                   jax.ShapeDtypeStruct((r, nk), BF16)],
        compiler_params=_params(("parallel",)),
        name="in_proj_odd",
    )(h2, g, wq, wk, wv, qn, kn, cos, sin)


def _attn_kernel(q_ref, k_ref, v_ref, o_ref, kp_ref, vaug_ref, kmax_ref, qs_ref, b_ref, acc_ref, *, tqb, tk, lp):
    sub = LANE
    n_sub = tqb // sub
    lpk = kp_ref.shape[0]
    n_kc = lpk // tk
    rows_q = ATT_GROUP * sub
    ones_bf = jnp.ones((LANE, LANE), BF16)

    @pl.when(pl.program_id(2) == 0)
    def _():
        def fill(c, kmax):
            rows = pl.ds(pl.multiple_of(c * tqb, tqb), tqb)
            valid = (lax.broadcasted_iota(jnp.int32, (tqb, LANE), 0) + c * tqb) >= FRONT
            vaug_ref[rows, :ATT_HD] = jnp.where(valid, v_ref[0, rows, :], jnp.zeros((tqb, ATT_HD), BF16))
            vaug_ref[rows, ATT_HD:] = jnp.where(valid, 1.0, 0.0).astype(BF16)
            kc = k_ref[0, rows, :]
            kp_ref[rows, :] = kc
            kf = kc.astype(F32)
            n2 = jnp.where(valid, _lane_sum(kf * kf, ones_bf), 0.0)
            return jnp.maximum(kmax, jnp.max(n2, axis=0, keepdims=True))

        kmax = lax.fori_loop(0, lp // tqb, fill, jnp.zeros((1, LANE), F32))
        kmax_ref[...] = jnp.broadcast_to(kmax, kmax_ref.shape)
        if lpk > lp:
            kp_ref[lp:, :] = jnp.zeros((lpk - lp, ATT_HD), BF16)
            vaug_ref[lp:, :] = jnp.zeros((lpk - lp, 2 * ATT_HD), BF16)

    kmax2 = kmax_ref[0:1, :]
    qss = []
    for si in range(n_sub):
        q4 = q_ref[0, si * sub:(si + 1) * sub, :]
        qss.append(jnp.concatenate([q4[:, g * ATT_HD:(g + 1) * ATT_HD] for g in range(ATT_GROUP)], axis=0))
        qs_ref[si] = qss[si]
        acc_ref[si] = jnp.zeros((rows_q, 2 * ATT_HD), F32)
    qn2s = [_lane_sum(qs.astype(F32) * qs.astype(F32), ones_bf) for qs in qss]
    for si, qn2 in enumerate(qn2s):
        b_ref[si] = jnp.sqrt(qn2 * kmax2) * (1.0 + 2.0 ** -10)

    def fast(c, _):
        rows = pl.ds(pl.multiple_of(c * tk, tk), tk)
        kc = kp_ref[rows, :]
        va = vaug_ref[rows, :]
        for si in range(n_sub):
            s = _dot_nt(qs_ref[si], kc)
            p = jnp.exp2(s - pltpu.repeat(b_ref[si], tk // LANE, axis=1)).astype(BF16)
            acc_ref[si] += _dot(p, va)
        return 0

    lax.fori_loop(0, n_kc, fast, 0, unroll=True)

    def write(si, acc):
        out = (acc[:, :ATT_HD] / acc[:, ATT_HD:]).astype(BF16)
        start = si * sub if isinstance(si, int) else pl.multiple_of(si * sub, sub)
        for g in range(ATT_GROUP):
            o_ref[0, pl.ds(start, sub), g * ATT_HD:(g + 1) * ATT_HD] = out[g * sub:(g + 1) * sub]

    l_min = None
    for si in range(n_sub):
        acc = acc_ref[si]
        write(si, acc)
        m = jnp.min(acc[:, ATT_HD:])
        l_min = m if l_min is None else jnp.minimum(l_min, m)

    @pl.when(jnp.logical_not(l_min >= SOFTMAX_MIN_DENOM))
    def _():
        col = lax.broadcasted_iota(jnp.int32, (rows_q, tk), 1)

        def redo(si, _):
            qs = qs_ref[si]

            def step(c, carry):
                m, acc = carry
                rows = pl.ds(pl.multiple_of(c * tk, tk), tk)
                key = col + c * tk
                s = jnp.where((key >= FRONT) & (key < lp), _dot_nt(qs, kp_ref[rows, :]), -jnp.inf)
                m_new = jnp.maximum(m, jnp.max(s, axis=1, keepdims=True))
                p = jnp.exp2(s - m_new).astype(BF16)
                return m_new, jnp.exp2(m - m_new) * acc + _dot(p, vaug_ref[rows, :])

            init = (jnp.full((rows_q, 1), -jnp.inf, F32), jnp.zeros((rows_q, 2 * ATT_HD), F32))
            _, acc = lax.fori_loop(0, n_kc, step, init)
            write(si, acc)
            return 0

        lax.fori_loop(0, n_sub, redo, 0)


def _attention(q, k, v, tqb):
    b, lp, _ = q.shape
    gw = ATT_GROUP * ATT_HD
    n_sub = tqb // LANE
    rows_q = ATT_GROUP * LANE
    qspec = pl.BlockSpec((1, tqb, gw), lambda bi, hi, i: (bi, i, hi))
    kvspec = pl.BlockSpec((1, lp, ATT_HD), lambda bi, hi, i: (bi, 0, hi))
    tk = min(3 * MXU_TILE, -(-lp // MXU_TILE) * MXU_TILE)
    lpk = -(-lp // tk) * tk
    return pl.pallas_call(
        functools.partial(_attn_kernel, tqb=tqb, tk=tk, lp=lp),
        grid=(b, ATT_KV_HEADS, lp // tqb),
        in_specs=[qspec, kvspec, kvspec],
        out_specs=qspec,
        out_shape=jax.ShapeDtypeStruct((b, lp, ATT_HEADS * ATT_HD), BF16),
        scratch_shapes=[pltpu.VMEM((lpk, ATT_HD), BF16), pltpu.VMEM((lpk, 2 * ATT_HD), BF16),
                        pltpu.VMEM((8, LANE), F32),
                        pltpu.VMEM((n_sub, rows_q, ATT_HD), BF16), pltpu.VMEM((n_sub, rows_q, LANE), F32),
                        pltpu.VMEM((n_sub, rows_q, 2 * ATT_HD), F32)],
        compiler_params=_params(("parallel", "parallel", "arbitrary")),
        name="gqa_attention",
    )(q, k, v)


def _rope_tables(lp):
    s = lp - LANE
    pos = np.arange(s)
    freqs = jnp.asarray(ROPE_THETA, F32) ** (-(jnp.arange(ATT_HD // 4, dtype=F32) / (ATT_HD // 4)))
    row = jnp.asarray(pos // GRID_W, F32)
    colp = jnp.asarray(pos % GRID_W, F32)
    ang = jnp.concatenate([row[:, None] * freqs, colp[:, None] * freqs], axis=-1)
    ang = jnp.concatenate([jnp.zeros((LANE, ATT_HD // 2), F32), ang], axis=0)
    c, sn = jnp.cos(ang), jnp.sin(ang)
    return jnp.concatenate([c, c], axis=-1), jnp.concatenate([-sn, sn], axis=-1)


def _trunk(xa, xb, meta_tokens, mix_norm, mlp_norm, w_in_even, pool_w, pool_scale, conv_qkv, a_log, dt_bias,
           delta_norm, w_out_even, w_in_odd, q_norm, k_norm, w_out_odd, w_mlp_in, w_mlp_out):
    groups = (xa.shape[0], xb.shape[0])
    b = sum(groups)
    s = xa.shape[1]
    lp = s + LANE
    tt = _token_tile(lp)
    r = b * lp

    o1 = POOL_WIDTH
    o2 = o1 + 2 * DN_QK + DN_V
    o3 = o2 + DN_V
    w_in = w_in_even[0]
    wb = jnp.zeros((D_MODEL, LANE), F32).at[:, :4 * DN_HEADS].set(w_in[:, o3:])
    meta_pad = jnp.concatenate([jnp.zeros((FRONT, D_MODEL), F32), meta_tokens], axis=0)
    h2, up, uq, z, ba = _in_even(xa, xb, meta_pad, mix_norm[0][None], w_in[:, :o1].astype(BF16),
                                 w_in[:, o1:o2].astype(BF16), w_in[:, o2:o3].astype(BF16), wb.astype(BF16), tt)
    gate_p = jnp.zeros((8, LANE), F32)
    gate_p = gate_p.at[0, 2 * DN_HEADS:4 * DN_HEADS].set(-jnp.exp(a_log[0].reshape(-1)))
    gate_p = gate_p.at[1, 2 * DN_HEADS:4 * DN_HEADS].set(dt_bias[0].reshape(-1))
    conv_w = jnp.zeros((8, 2 * DN_QK + DN_V), F32).at[:DN_CONV].set(conv_qkv[0])
    seq = lambda a: a.reshape(b, lp, a.shape[-1])
    y_pool, q, k, v, gt = _prep(seq(up), seq(uq), seq(ba), pool_w[0].astype(BF16), pool_scale[0][None], conv_w,
                                gate_p, tt)
    w, u, qd, kd, at, gl = _delta_intra(q, k, v, gt, tt)
    o_f, o_b = _delta_inter(w, u, qd, kd, at, gl, tt)
    wo = w_out_even[0].astype(BF16)
    flat = lambda a: a.reshape(r, a.shape[-1])
    h2 = _out_mlp_even(flat(y_pool), flat(o_f), flat(o_b), z, delta_norm[0][None], wo[:POOL_WIDTH], wo[POOL_WIDTH:],
                       h2, mlp_norm[0][None], w_mlp_in[0].astype(BF16), w_mlp_out[0].astype(BF16), tt)

    nq, nk = ATT_HEADS * ATT_HD, ATT_KV_HEADS * ATT_HD
    w_in = w_in_odd[0].astype(BF16)
    cos, sin = _rope_tables(lp)
    qa, ka, va = _in_odd(h2, mix_norm[1][None], w_in[:, :nq], w_in[:, nq:nq + nk], w_in[:, nq + nk:],
                         q_norm[0][None], k_norm[0][None], cos, sin, tt)
    att = _attention(seq(qa), seq(ka), seq(va), tt)
    outs, b0 = [], 0
    for nb in groups:
        outs.append(_out_mlp_last(att, w_out_odd[0].astype(BF16), h2.reshape(b, lp, D_MODEL), mlp_norm[1][None],
                                  w_mlp_in[1].astype(BF16), w_mlp_out[1].astype(BF16), tt, b0, nb))
        b0 += nb
    return tuple(outs)


def kernel(x_prompt, x_sample, meta_tokens, mix_norm, mlp_norm, w_in_even, pool_w, pool_scale, conv_qkv, a_log,
           dt_bias, delta_norm, w_out_even, w_in_odd, q_norm, k_norm, w_out_odd, w_mlp_in, w_mlp_out):
    return _trunk(x_prompt, x_sample, meta_tokens, mix_norm, mlp_norm, w_in_even, pool_w, pool_scale, conv_qkv, a_log,
                  dt_bias, delta_norm, w_out_even, w_in_odd, q_norm, k_norm, w_out_odd, w_mlp_in, w_mlp_out)
```
